```python
import math
import jax, jax.numpy as jnp
from jax import lax
import numpy as np

D_MODEL = 4096
BATCH = 4
SEQ = 2048
DEPTH = 1
DEC_BATCH = 128
DEC_SEQ = 1
PAST_LEN = 16384
PAGE_SIZE = 128

CONV_WIDTH = D_MODEL // 2
CONV_K = 31
SSM_WIDTH = D_MODEL // 2
SSM_GROUP = 16
SSM_GROUPS = SSM_WIDTH // SSM_GROUP
SSM_STATE = 64
D_FF = 256 * ((8 * D_MODEL // 3 + 255) // 256)
FFN_K = 3
IN_WIDTH = 2 * CONV_WIDTH + SSM_WIDTH + 2 * D_MODEL
EPS = 1e-6

kernel_name = "gated_conformer_conv_s5_convffn_step"


def rmsnorm(x, g):
    xf = x.astype(jnp.float32)
    r = xf * lax.rsqrt(jnp.mean(xf * xf, axis=-1, keepdims=True) + EPS)
    return (r * g.astype(jnp.float32)).astype(x.dtype)


def layernorm(x, g, b):
    xf = x.astype(jnp.float32)
    mu = jnp.mean(xf, axis=-1, keepdims=True)
    var = jnp.mean(jnp.square(xf - mu), axis=-1, keepdims=True)
    r = (xf - mu) * lax.rsqrt(var + EPS)
    return (r * g.astype(jnp.float32) + b.astype(jnp.float32)).astype(x.dtype)


def causal_dwconv(x, buf, w, b):
    k = w.shape[0]
    xp = jnp.concatenate([buf.astype(x.dtype), x], axis=1)
    out = lax.conv_general_dilated(
        xp, w[:, None, :].astype(x.dtype), window_strides=(1,), padding='VALID',
        dimension_numbers=('NWC', 'WIO', 'NWC'), feature_group_count=x.shape[-1])
    return out + b.astype(x.dtype), xp[:, xp.shape[1] - (k - 1):]


def _complex_combine(e1, e2):
    a1r, a1i, b1r, b1i = e1
    a2r, a2i, b2r, b2i = e2
    return (a2r * a1r - a2i * a1i,
            a2r * a1i + a2i * a1r,
            a2r * b1r - a2i * b1i + b2r,
            a2r * b1i + a2i * b1r + b2i)


def s5_layer(u, h0_re, h0_im, lam_re, lam_im, log_dt, b_re, b_im, c_re, c_im, d_skip):
    n, s, _ = u.shape
    f32 = jnp.float32
    uf = u.astype(f32)
    ug = uf.reshape(n, s, SSM_GROUPS, SSM_GROUP)
    lr, li = lam_re.astype(f32), lam_im.astype(f32)
    dt = jnp.exp(log_dt.astype(f32))[:, None]
    mag = jnp.exp(lr * dt)
    ab_re, ab_im = mag * jnp.cos(li * dt), mag * jnp.sin(li * dt)
    den = lr * lr + li * li
    nr, ni = ab_re - 1.0, ab_im
    f_re = (nr * lr + ni * li) / den
    f_im = (ni * lr - nr * li) / den
    br, bi = b_re.astype(f32), b_im.astype(f32)
    bb_re = f_re[..., None] * br - f_im[..., None] * bi
    bb_im = f_re[..., None] * bi + f_im[..., None] * br
    bu_re = jnp.einsum('nsgh,gph->nsgp', ug, bb_re)
    bu_im = jnp.einsum('nsgh,gph->nsgp', ug, bb_im)
    a_re = jnp.broadcast_to(ab_re, bu_re.shape)
    a_im = jnp.broadcast_to(ab_im, bu_im.shape)
    acc_r, acc_i, hr, hi = lax.associative_scan(
        _complex_combine, (a_re, a_im, bu_re, bu_im), axis=1)
    h0r = h0_re.astype(f32)[:, None]
    h0i = h0_im.astype(f32)[:, None]
    hr = hr + acc_r * h0r - acc_i * h0i
    hi = hi + acc_r * h0i + acc_i * h0r
    y = (jnp.einsum('nsgp,ghp->nsgh', hr, c_re.astype(f32))
         - jnp.einsum('nsgp,ghp->nsgh', hi, c_im.astype(f32)))
    y = y.reshape(n, s, SSM_WIDTH) + d_skip.astype(f32) * uf
    return y.astype(u.dtype), hr[:, -1], hi[:, -1]


def mixer_block(xn, conv_buf, h_re, h_im, p):
    proj = xn @ p['w_in']
    a_in, a_gate, s_in, g_a, g_b = jnp.split(
        proj, [CONV_WIDTH, 2 * CONV_WIDTH, 2 * CONV_WIDTH + SSM_WIDTH,
               2 * CONV_WIDTH + SSM_WIDTH + D_MODEL], axis=-1)
    glu = a_in * jax.nn.sigmoid(a_gate)
    c, new_conv = causal_dwconv(glu, conv_buf, p['conv_w'], p['conv_b'])
    c = jax.nn.silu(layernorm(c, p['ln_g'], p['ln_b']))
    y_a = c @ p['w_conv_out']
    s, nh_re, nh_im = s5_layer(s_in, h_re, h_im, p['lam_re'], p['lam_im'], p['log_dt'],
                               p['b_re'], p['b_im'], p['c_re'], p['c_im'], p['d_skip'])
    sg = jax.nn.gelu(s)
    y_b = (sg * jax.nn.sigmoid(sg @ p['w_glu'])) @ p['w_ssm_out']
    merged = jax.nn.sigmoid(g_a) * y_a + jax.nn.sigmoid(g_b) * y_b
    return merged @ p['w_o'], new_conv, nh_re, nh_im


def ffn_block(xn, buf, p):
    gate, val = jnp.split(xn @ p['w_up'], [D_FF], axis=-1)
    gc, new_buf = causal_dwconv(gate, buf, p['ffn_conv_w'], p['ffn_conv_b'])
    return (jax.nn.silu(gc) * val) @ p['w_down'], new_buf


def layer(x, conv_buf, h_re, h_im, ffn_buf, p):
    m, new_conv, nh_re, nh_im = mixer_block(rmsnorm(x, p['norm_mix_g']), conv_buf, h_re, h_im, p)
    x = x + m
    f, new_ffn = ffn_block(rmsnorm(x, p['norm_ffn_g']), ffn_buf, p)
    return x + f, new_conv, nh_re, nh_im, new_ffn


def setup_inputs(seed: int = 0) -> dict:
    key = jax.random.key(seed)
    ks = jax.random.split(key, 40)
    f32 = jnp.float32

    def nrm(k, shape, scale):
        return jax.random.normal(k, shape, f32) * scale

    L = DEPTH
    lam_im = (math.pi * jnp.arange(SSM_STATE, dtype=f32))[None, None, :] + nrm(ks[10], (L, SSM_GROUPS, SSM_STATE), 0.01)
    return {
        'x_prompt': nrm(ks[0], (BATCH, SEQ, D_MODEL), 1.0),
        'x_sample': nrm(ks[1], (DEC_BATCH, DEC_SEQ, D_MODEL), 1.0),
        'state_conv': nrm(ks[2], (L, DEC_BATCH, CONV_K - 1, CONV_WIDTH), 0.5),
        'state_ssm_re': nrm(ks[3], (L, DEC_BATCH, SSM_GROUPS, SSM_STATE), 0.3),
        'state_ssm_im': nrm(ks[4], (L, DEC_BATCH, SSM_GROUPS, SSM_STATE), 0.3),
        'state_ffn_conv': nrm(ks[5], (L, DEC_BATCH, FFN_K - 1, D_FF), 1.0),
        'norm_mix_g': 1.0 + nrm(ks[6], (L, D_MODEL), 0.02),
        'w_in': nrm(ks[7], (L, D_MODEL, IN_WIDTH), D_MODEL ** -0.5),
        'conv_w': nrm(ks[8], (L, CONV_K, CONV_WIDTH), CONV_K ** -0.5),
        'conv_b': nrm(ks[9], (L, CONV_WIDTH), 0.02),
        'ln_g': 1.0 + nrm(ks[11], (L, CONV_WIDTH), 0.02),
        'ln_b': nrm(ks[12], (L, CONV_WIDTH), 0.02),
        'w_conv_out': nrm(ks[13], (L, CONV_WIDTH, D_MODEL), CONV_WIDTH ** -0.5),
        'lam_re': -0.5 + nrm(ks[14], (L, SSM_GROUPS, SSM_STATE), 0.01),
        'lam_im': lam_im,
        'log_dt': jax.random.uniform(ks[15], (L, SSM_GROUPS), f32, math.log(1e-3), math.log(1e-1)),
        'b_re': nrm(ks[16], (L, SSM_GROUPS, SSM_STATE, SSM_GROUP), (2.0 * SSM_GROUP) ** -0.5),
        'b_im': nrm(ks[17], (L, SSM_GROUPS, SSM_STATE, SSM_GROUP), (2.0 * SSM_GROUP) ** -0.5),
        'c_re': nrm(ks[18], (L, SSM_GROUPS, SSM_GROUP, SSM_STATE), (2.0 * SSM_STATE) ** -0.5),
        'c_im': nrm(ks[19], (L, SSM_GROUPS, SSM_GROUP, SSM_STATE), (2.0 * SSM_STATE) ** -0.5),
        'd_skip': nrm(ks[20], (L, SSM_WIDTH), 1.0),
        'w_glu': nrm(ks[21], (L, SSM_WIDTH, SSM_WIDTH), SSM_WIDTH ** -0.5),
        'w_ssm_out': nrm(ks[22], (L, SSM_WIDTH, D_MODEL), SSM_WIDTH ** -0.5),
        'w_o': nrm(ks[23], (L, D_MODEL, D_MODEL), D_MODEL ** -0.5),
        'norm_ffn_g': 1.0 + nrm(ks[24], (L, D_MODEL), 0.02),
        'w_up': nrm(ks[25], (L, D_MODEL, 2 * D_FF), D_MODEL ** -0.5),
        'ffn_conv_w': nrm(ks[26], (L, FFN_K, D_FF), FFN_K ** -0.5),
        'ffn_conv_b': nrm(ks[27], (L, D_FF), 0.02),
        'w_down': nrm(ks[28], (L, D_FF, D_MODEL), D_FF ** -0.5),
        'final_norm_g': 1.0 + nrm(ks[29], (D_MODEL,), 0.02),
    }


def reference(x_prompt, x_sample, state_conv, state_ssm_re, state_ssm_im, state_ffn_conv,
              norm_mix_g, w_in, conv_w, conv_b, ln_g, ln_b, w_conv_out,
              lam_re, lam_im, log_dt, b_re, b_im, c_re, c_im, d_skip, w_glu, w_ssm_out, w_o,
              norm_ffn_g, w_up, ffn_conv_w, ffn_conv_b, w_down, final_norm_g):
    xp, xs = x_prompt, x_sample
    sdt = jnp.float32
    conv_p, conv_s, ssr_p, ssi_p, ssr_s, ssi_s, ffn_p, ffn_s = [], [], [], [], [], [], [], []
    for l in range(DEPTH):
        p = dict(norm_mix_g=norm_mix_g[l], w_in=w_in[l], conv_w=conv_w[l], conv_b=conv_b[l],
                 ln_g=ln_g[l], ln_b=ln_b[l], w_conv_out=w_conv_out[l], lam_re=lam_re[l],
                 lam_im=lam_im[l], log_dt=log_dt[l], b_re=b_re[l], b_im=b_im[l], c_re=c_re[l],
                 c_im=c_im[l], d_skip=d_skip[l], w_glu=w_glu[l], w_ssm_out=w_ssm_out[l], w_o=w_o[l],
                 norm_ffn_g=norm_ffn_g[l], w_up=w_up[l], ffn_conv_w=ffn_conv_w[l],
                 ffn_conv_b=ffn_conv_b[l], w_down=w_down[l])
        nb = xp.shape[0]
        xp, c1, r1, i1, f1 = layer(
            xp,
            jnp.zeros((nb, CONV_K - 1, CONV_WIDTH), xp.dtype),
            jnp.zeros((nb, SSM_GROUPS, SSM_STATE), sdt),
            jnp.zeros((nb, SSM_GROUPS, SSM_STATE), sdt),
            jnp.zeros((nb, FFN_K - 1, D_FF), xp.dtype), p)
        xs, c2, r2, i2, f2 = layer(xs, state_conv[l], state_ssm_re[l], state_ssm_im[l],
                                   state_ffn_conv[l], p)
        conv_p.append(c1); ssr_p.append(r1); ssi_p.append(i1); ffn_p.append(f1)
        conv_s.append(c2); ssr_s.append(r2); ssi_s.append(i2); ffn_s.append(f2)
    y_prompt = rmsnorm(xp, final_norm_g)
    y_sample = rmsnorm(xs, final_norm_g)
    return (y_prompt, y_sample,
            jnp.stack(conv_p), jnp.stack(conv_s),
            jnp.stack(ssr_p), jnp.stack(ssi_p), jnp.stack(ssr_s), jnp.stack(ssi_s),
            jnp.stack(ffn_p), jnp.stack(ffn_s))
```

```python
import functools
import math

import jax
import jax.numpy as jnp
from jax import lax
from jax.experimental import pallas as pl
from jax.experimental.pallas import tpu as pltpu

D_MODEL = 4096
BATCH = 4
SEQ = 2048
DEC_BATCH = 128
CONV_WIDTH = D_MODEL // 2
CONV_K = 31
SSM_WIDTH = D_MODEL // 2
SSM_GROUP = 16
SSM_GROUPS = SSM_WIDTH // SSM_GROUP
SSM_STATE = 64
D_FF = 11008
FFN_K = 3
EPS = 1e-6

P_ROWS = BATCH * SEQ
S_ROWS = DEC_BATCH
N_ROWS = P_ROWS + S_ROWS

V7X_SUBLANES = 8
V7X_LANES = 128
V7X_VMEM_LIMIT = 56 * 1024 * 1024

TM = 1024
NPT = P_ROWS // TM
S_BLK = P_ROWS // S_ROWS

BF16 = jnp.bfloat16
F32 = jnp.float32


def _params(n_grid_dims):
    return pltpu.CompilerParams(
        dimension_semantics=("arbitrary",) * n_grid_dims,
        vmem_limit_bytes=V7X_VMEM_LIMIT)


def _dot(a, b):
    return jnp.dot(a, b, preferred_element_type=F32)


def _rms(x, g):
    return x * lax.rsqrt(jnp.mean(x * x, axis=-1, keepdims=True) + EPS) * g


def _prenorm_kernel(xp_ref, xs_ref, g_ref, o_ref, *, n_prompt_tiles):
    i = pl.program_id(0)

    @pl.when(i < n_prompt_tiles)
    def _():
        o_ref[...] = _rms(xp_ref[...], g_ref[...]).astype(o_ref.dtype)

    @pl.when(i == n_prompt_tiles)
    def _():
        o_ref[0:S_ROWS, :] = _rms(xs_ref[...], g_ref[...]).astype(o_ref.dtype)


def prenorm(xp, xs, g):
    tr = 512
    npt = P_ROWS // tr
    return pl.pallas_call(
        functools.partial(_prenorm_kernel, n_prompt_tiles=npt),
        out_shape=jax.ShapeDtypeStruct((N_ROWS, D_MODEL), BF16),
        grid=(npt + 1,),
        in_specs=[
            pl.BlockSpec((tr, D_MODEL), lambda i: (jnp.minimum(i, npt - 1), 0)),
            pl.BlockSpec((S_ROWS, D_MODEL), lambda i: (0, 0)),
            pl.BlockSpec((1, D_MODEL), lambda i: (0, 0)),
        ],
        out_specs=pl.BlockSpec((tr, D_MODEL), lambda i: (i, 0)),
        compiler_params=_params(1),
        name="prenorm",
    )(xp, xs, g)


def _rownorm_kernel(x_ref, g_ref, o_ref):
    o_ref[...] = _rms(x_ref[...], g_ref[...]).astype(o_ref.dtype)


def rownorm(x, g, out_dtype, rows, row_block, first_block=0):
    n = pl.cdiv(rows, row_block)
    return pl.pallas_call(
        _rownorm_kernel,
        out_shape=jax.ShapeDtypeStruct((rows, D_MODEL), out_dtype),
        grid=(n,),
        in_specs=[
            pl.BlockSpec((row_block, D_MODEL), lambda i: (i + first_block, 0)),
            pl.BlockSpec((1, D_MODEL), lambda i: (0, 0)),
        ],
        out_specs=pl.BlockSpec((row_block, D_MODEL), lambda i: (i, 0)),
        compiler_params=_params(1),
        name="rownorm",
    )(x, g)


def _row_tiles(body):
    i = pl.program_id(0)

    @pl.when(i < NPT)
    def _():
        body(TM)

    @pl.when(i == NPT)
    def _():
        body(S_ROWS)


def _inproj_glu_kernel(a_ref, wa_ref, wg_ref, o_ref):
    def body(rows):
        a = a_ref[0:rows, :]
        lin = _dot(a, wa_ref[...].astype(BF16))
        gate = _dot(a, wg_ref[...].astype(BF16))
        o_ref[0:rows, :] = lin * jax.nn.sigmoid(gate)
    _row_tiles(body)


def inproj_glu(xn, w_in):
    tn = 256
    nj = CONV_WIDTH // tn
    return pl.pallas_call(
        _inproj_glu_kernel,
        out_shape=jax.ShapeDtypeStruct((N_ROWS, CONV_WIDTH), F32),
        grid=(NPT + 1, nj),
        in_specs=[
            pl.BlockSpec((TM, D_MODEL), lambda i, j: (i, 0)),
            pl.BlockSpec((D_MODEL, tn), lambda i, j: (0, j)),
            pl.BlockSpec((D_MODEL, tn), lambda i, j: (0, j + nj)),
        ],
        out_specs=pl.BlockSpec((TM, tn), lambda i, j: (i, j)),
        compiler_params=_params(2),
        name="inproj_glu",
    )(xn, w_in, w_in)


def _inproj_cols_kernel(a_ref, w_ref, o_ref, *, squash):
    def body(rows):
        r = _dot(a_ref[0:rows, :], w_ref[...].astype(BF16))
        o_ref[0:rows, :] = jax.nn.sigmoid(r) if squash else r
    _row_tiles(body)


def inproj_cols(xn, w_in, col0, width, squash, name):
    tn = 512
    return pl.pallas_call(
        functools.partial(_inproj_cols_kernel, squash=squash),
        out_shape=jax.ShapeDtypeStruct((N_ROWS, width), F32),
        grid=(NPT + 1, width // tn),
        in_specs=[
            pl.BlockSpec((TM, D_MODEL), lambda i, j: (i, 0)),
            pl.BlockSpec((D_MODEL, tn), lambda i, j: (0, j + col0 // tn)),
        ],
        out_specs=pl.BlockSpec((TM, tn), lambda i, j: (i, j)),
        compiler_params=_params(2),
        name=name,
    )(xn, w_in)


CONV_TT = 256
CONV_HALO = 32
CONV_RC = 32
CONV_LC = 256
CONV_PIECE = 32


def _ln_swish(c, g, b):
    mu = jnp.mean(c, axis=-1, keepdims=True)
    d = c - mu
    var = jnp.mean(d * d, axis=-1, keepdims=True)
    r = d * lax.rsqrt(var + EPS) * g + b
    return r * jax.nn.sigmoid(r)


def _conv_prompt_kernel(x_ref, w_ref, cb_ref, g_ref, b_ref, o_ref, st_ref, xs_ref, cbuf_ref):
    t = pl.program_id(1)
    nt = pl.num_programs(1)
    sub = V7X_SUBLANES

    @pl.when(t == 0)
    def _():
        xs_ref[0, 0:CONV_HALO, :] = jnp.zeros((CONV_HALO, CONV_WIDTH), F32)
        xs_ref[0, CONV_HALO + CONV_TT:CONV_HALO + CONV_TT + sub, :] = jnp.zeros((sub, CONV_WIDTH), F32)

    xs_ref[0, CONV_HALO:CONV_HALO + CONV_TT, :] = x_ref[...]

    def shift_piece(p, carry):
        r = pl.multiple_of(p * CONV_PIECE, CONV_PIECE)
        piece = xs_ref[0, pl.ds(r, CONV_PIECE + sub), :]
        for m in range(1, sub):
            rolled = pltpu.roll(piece, CONV_PIECE + sub - m, axis=0)
            xs_ref[m, pl.ds(r, CONV_PIECE), :] = rolled[0:CONV_PIECE]
        return carry

    lax.fori_loop(0, (CONV_HALO + CONV_TT) // CONV_PIECE, shift_piece, 0)

    off = CONV_HALO - (CONV_K - 1)

    def chunk(c, carry):
        r0 = pl.multiple_of(c * CONV_RC, CONV_RC)
        for l0 in range(0, CONV_WIDTH, CONV_LC):
            lanes = slice(l0, l0 + CONV_LC)
            acc = jnp.zeros((CONV_RC, CONV_LC), F32)
            for k in range(CONV_K):
                o = off + k
                win = xs_ref[o % sub, pl.ds(r0 + (o // sub) * sub, CONV_RC), lanes]
                acc = acc + win * w_ref[k:k + 1, lanes]
            cbuf_ref[:, lanes] = acc + cb_ref[:, lanes]
        o_ref[pl.ds(r0, CONV_RC), :] = _ln_swish(cbuf_ref[...], g_ref[...], b_ref[...]).astype(o_ref.dtype)
        return carry

    lax.fori_loop(0, CONV_TT // CONV_RC, chunk, 0)

    @pl.when(t == nt - 1)
    def _():
        st_ref[0] = xs_ref[off, CONV_TT:CONV_TT + CONV_K - 1, :]

    xs_ref[0, 0:CONV_HALO, :] = xs_ref[0, CONV_TT:CONV_TT + CONV_HALO, :]


def conv_prompt(glu, conv_w, conv_b, ln_g, ln_b):
    nt = SEQ // CONV_TT
    vec = pl.BlockSpec((1, CONV_WIDTH), lambda b, t: (0, 0))
    return pl.pallas_call(
        _conv_prompt_kernel,
        out_shape=(jax.ShapeDtypeStruct((P_ROWS, CONV_WIDTH), BF16),
                   jax.ShapeDtypeStruct((BATCH, CONV_K - 1, CONV_WIDTH), F32)),
        grid=(BATCH, nt),
        in_specs=[
            pl.BlockSpec((CONV_TT, CONV_WIDTH), lambda b, t: (b * nt + t, 0)),
            pl.BlockSpec((CONV_K, CONV_WIDTH), lambda b, t: (0, 0)),
            vec, vec, vec,
        ],
        out_specs=(pl.BlockSpec((CONV_TT, CONV_WIDTH), lambda b, t: (b * nt + t, 0)),
                   pl.BlockSpec((1, CONV_K - 1, CONV_WIDTH), lambda b, t: (b, 0, 0))),
        scratch_shapes=[
            pltpu.VMEM((V7X_SUBLANES, CONV_HALO + CONV_TT + V7X_SUBLANES, CONV_WIDTH), F32),
            pltpu.VMEM((CONV_RC, CONV_WIDTH), F32),
        ],
        compiler_params=_params(2),
        name="conv_prompt",
    )(glu, conv_w, conv_b, ln_g, ln_b)


CONV_SB = 16


def _conv_sample_kernel(x_ref, st_ref, w_ref, cb_ref, g_ref, b_ref, o_ref, nst_ref):
    c = CONV_WIDTH
    x = x_ref[...]
    acc = x * w_ref[CONV_K - 1:CONV_K, :] + cb_ref[...]
    for k in range(CONV_K - 1):
        acc = acc + st_ref[:, k * c:(k + 1) * c] * w_ref[k:k + 1, :]
    o_ref[...] = _ln_swish(acc, g_ref[...], b_ref[...]).astype(o_ref.dtype)
    nst_ref[:, 0:(CONV_K - 2) * c] = st_ref[:, c:(CONV_K - 1) * c]
    nst_ref[:, (CONV_K - 2) * c:(CONV_K - 1) * c] = x


def conv_sample(glu, state2d, conv_w, conv_b, ln_g, ln_b):
    vec = pl.BlockSpec((1, CONV_WIDTH), lambda i: (0, 0))
    first = P_ROWS // CONV_SB
    st_w = (CONV_K - 1) * CONV_WIDTH
    return pl.pallas_call(
        _conv_sample_kernel,
        out_shape=(jax.ShapeDtypeStruct((S_ROWS, CONV_WIDTH), BF16),
                   jax.ShapeDtypeStruct((S_ROWS, st_w), F32)),
        grid=(S_ROWS // CONV_SB,),
        in_specs=[
            pl.BlockSpec((CONV_SB, CONV_WIDTH), lambda i: (first + i, 0)),
            pl.BlockSpec((CONV_SB, st_w), lambda i: (i, 0)),
            pl.BlockSpec((CONV_K, CONV_WIDTH), lambda i: (0, 0)),
            vec, vec, vec,
        ],
        out_specs=(pl.BlockSpec((CONV_SB, CONV_WIDTH), lambda i: (i, 0)),
                   pl.BlockSpec((CONV_SB, st_w), lambda i: (i, 0))),
        compiler_params=_params(1),
        name="conv_sample",
    )(glu, state2d, conv_w, conv_b, ln_g, ln_b)


S5_GB = 16
S5_CH = S5_GB * SSM_GROUP
S5_ST = S5_GB * SSM_STATE
S5_NB = SSM_GROUPS // S5_GB
S5_SEG = V7X_SUBLANES
S5_SEGLEN = SEQ // S5_SEG
S5_TC = 64
S5_LH = S5_CH // V7X_LANES


def _s5_discretize(lr_ref, li_ref, ldt_ref):
    lr, li = lr_ref[0], li_ref[0]
    dt = jnp.exp(ldt_ref[0])
    mag = jnp.exp(lr * dt)
    a_re, a_im = mag * jnp.cos(li * dt), mag * jnp.sin(li * dt)
    den = lr * lr + li * li
    nr, ni = a_re - 1.0, a_im
    f_re = (nr * lr + ni * li) / den
    f_im = (ni * lr - nr * li) / den
    return a_re, a_im, f_re, f_im


def _s5_fill_weights(f_re, f_im, bdre_ref, bdim_ref, cdre_ref, cdim_ref, bb_ref, cc_ref):
    bdre, bdim = bdre_ref[0], bdim_ref[0]
    bb_ref[:, 0:S5_ST] = (f_re * bdre - f_im * bdim).astype(BF16)
    bb_ref[:, S5_ST:2 * S5_ST] = (f_re * bdim + f_im * bdre).astype(BF16)
    cc_ref[0:S5_ST, :] = cdre_ref[0].astype(BF16)
    cc_ref[S5_ST:2 * S5_ST, :] = (-cdim_ref[0]).astype(BF16)


def _s5_prompt_kernel(u_ref, lr_ref, li_ref, ldt_ref, bdre_ref, bdim_ref, cdre_ref, cdim_ref, d_ref,
                      sg_ref, hre_ref, him_ref, bb_ref, cc_ref, ul_ref, sgl_ref, lhs_ref, bu_ref, hch_ref):
    a_re, a_im, f_re, f_im = _s5_discretize(lr_ref, li_ref, ldt_ref)
    _s5_fill_weights(f_re, f_im, bdre_ref, bdim_ref, cdre_ref, cdim_ref, bb_ref, cc_ref)
    are8 = jnp.broadcast_to(a_re, (S5_SEG, S5_ST))
    aim8 = jnp.broadcast_to(a_im, (S5_SEG, S5_ST))
    d = d_ref[0]
    for hh in range(S5_LH):
        ul_ref[hh] = u_ref[:, hh * V7X_LANES:(hh + 1) * V7X_LANES]

    def run_pass(h_re, h_im, emit):
        def chunk(c, carry):
            hr, hi = carry
            t0 = c * S5_TC
            for tl in range(S5_TC):
                for hh in range(S5_LH):
                    lhs_ref[tl * S5_SEG:(tl + 1) * S5_SEG, hh * V7X_LANES:(hh + 1) * V7X_LANES] = (
                        ul_ref[hh, pl.ds(t0 + tl, S5_SEG, stride=S5_SEGLEN), :])
            bu_ref[...] = _dot(lhs_ref[...].astype(BF16), bb_ref[...])
            for tl in range(S5_TC):
                rows = slice(tl * S5_SEG, (tl + 1) * S5_SEG)
                br = bu_ref[rows, 0:S5_ST]
                bi = bu_ref[rows, S5_ST:2 * S5_ST]
                hr, hi = are8 * hr - aim8 * hi + br, are8 * hi + aim8 * hr + bi
                if emit:
                    hch_ref[rows, 0:S5_ST] = hr
                    hch_ref[rows, S5_ST:2 * S5_ST] = hi
            if emit:
                y = _dot(hch_ref[...].astype(BF16), cc_ref[...]) + d * lhs_ref[...]
                hch_ref[:, 0:S5_CH] = jax.nn.gelu(y)
                for tl in range(S5_TC):
                    for hh in range(S5_LH):
                        sgl_ref[hh, pl.ds(t0 + tl, S5_SEG, stride=S5_SEGLEN), :] = (
                            hch_ref[tl * S5_SEG:(tl + 1) * S5_SEG, hh * V7X_LANES:(hh + 1) * V7X_LANES])
            return hr, hi
        return lax.fori_loop(0, S5_SEGLEN // S5_TC, chunk, (h_re, h_im))

    zeros = jnp.zeros((S5_SEG, S5_ST), F32)
    e_re, e_im = run_pass(zeros, zeros, emit=False)

    p_re, p_im = a_re, a_im
    for _ in range(int(math.log2(S5_SEGLEN))):
        p_re, p_im = p_re * p_re - p_im * p_im, 2.0 * p_re * p_im
    seg = lax.broadcasted_iota(jnp.int32, (S5_SEG, S5_ST), 0)
    qr = qi = jnp.zeros((1, S5_ST), F32)
    in_re = in_im = zeros
    for s in range(1, S5_SEG):
        qr, qi = (p_re * qr - p_im * qi + e_re[s - 1:s, :],
                  p_re * qi + p_im * qr + e_im[s - 1:s, :])
        in_re = jnp.where(seg == s, qr, in_re)
        in_im = jnp.where(seg == s, qi, in_im)
    h_re, h_im = run_pass(in_re, in_im, emit=True)
    hre_ref[0] = h_re[S5_SEG - 1:S5_SEG, :]
    him_ref[0] = h_im[S5_SEG - 1:S5_SEG, :]
    for hh in range(S5_LH):
        sg_ref[:, hh * V7X_LANES:(hh + 1) * V7X_LANES] = sgl_ref[hh]


def _s5_param_specs(idx):
    def spec(shape):
        return pl.BlockSpec((1,) + shape, lambda *g: (idx(*g), 0, 0))
    return [spec((1, S5_ST)), spec((1, S5_ST)), spec((1, S5_ST)),
            spec((S5_CH, S5_ST)), spec((S5_CH, S5_ST)),
            spec((S5_ST, S5_CH)), spec((S5_ST, S5_CH)), spec((1, S5_CH))]


def s5_prompt(u, s5p):
    return pl.pallas_call(
        _s5_prompt_kernel,
        out_shape=(jax.ShapeDtypeStruct((P_ROWS, SSM_WIDTH), F32),
                   jax.ShapeDtypeStruct((BATCH, 1, SSM_GROUPS * SSM_STATE), F32),
                   jax.ShapeDtypeStruct((BATCH, 1, SSM_GROUPS * SSM_STATE), F32)),
        grid=(S5_NB, BATCH),
        in_specs=[pl.BlockSpec((SEQ, S5_CH), lambda j, b: (b, j))] + _s5_param_specs(lambda j, b: j),
        out_specs=(pl.BlockSpec((SEQ, S5_CH), lambda j, b: (b, j)),
                   pl.BlockSpec((1, 1, S5_ST), lambda j, b: (b, 0, j)),
                   pl.BlockSpec((1, 1, S5_ST), lambda j, b: (b, 0, j))),
        scratch_shapes=[
            pltpu.VMEM((S5_CH, 2 * S5_ST), BF16),
            pltpu.VMEM((2 * S5_ST, S5_CH), BF16),
            pltpu.VMEM((S5_LH, SEQ, V7X_LANES), F32),
            pltpu.VMEM((S5_LH, SEQ, V7X_LANES), F32),
            pltpu.VMEM((S5_TC * S5_SEG, S5_CH), F32),
            pltpu.VMEM((S5_TC * S5_SEG, 2 * S5_ST), F32),
            pltpu.VMEM((S5_TC * S5_SEG, 2 * S5_ST), F32),
        ],
        compiler_params=_params(2),
        name="s5_prompt",
    )(u, *s5p)


def _s5_sample_kernel(u_ref, h0re_ref, h0im_ref, lr_ref, li_ref, ldt_ref, bdre_ref, bdim_ref,
                      cdre_ref, cdim_ref, d_ref, sg_ref, hre_ref, him_ref, bb_ref, cc_ref):
    a_re, a_im, f_re, f_im = _s5_discretize(lr_ref, li_ref, ldt_ref)
    _s5_fill_weights(f_re, f_im, bdre_ref, bdim_ref, cdre_ref, cdim_ref, bb_ref, cc_ref)
    u = u_ref[...]
    bu = _dot(u.astype(BF16), bb_ref[...])
    h0r, h0i = h0re_ref[...], h0im_ref[...]
    hr = a_re * h0r - a_im * h0i + bu[:, 0:S5_ST]
    hi = a_re * h0i + a_im * h0r + bu[:, S5_ST:2 * S5_ST]
    hre_ref[...] = hr
    him_ref[...] = hi
    y = _dot(hr.astype(BF16), cc_ref[0:S5_ST, :]) + _dot(hi.astype(BF16), cc_ref[S5_ST:2 * S5_ST, :])
    sg_ref[...] = jax.nn.gelu(y + d_ref[0] * u)


def s5_sample(u, h0_re, h0_im, s5p):
    st = pl.BlockSpec((S_ROWS, S5_ST), lambda j: (0, j))
    return pl.pallas_call(
        _s5_sample_kernel,
        out_shape=(jax.ShapeDtypeStruct((S_ROWS, SSM_WIDTH), F32),
                   jax.ShapeDtypeStruct((S_ROWS, SSM_GROUPS * SSM_STATE), F32),
                   jax.ShapeDtypeStruct((S_ROWS, SSM_GROUPS * SSM_STATE), F32)),
        grid=(S5_NB,),
        in_specs=[pl.BlockSpec((S_ROWS, S5_CH), lambda j: (S_BLK, j)), st, st]
        + _s5_param_specs(lambda j: j),
        out_specs=(pl.BlockSpec((S_ROWS, S5_CH), lambda j: (0, j)), st, st),
        scratch_shapes=[
            pltpu.VMEM((S5_CH, 2 * S5_ST), BF16),
            pltpu.VMEM((2 * S5_ST, S5_CH), BF16),
        ],
        compiler_params=_params(1),
        name="s5_sample",
    )(u, h0_re, h0_im, *s5p)


def s5_block_params(lam_re, lam_im, log_dt, b_re, b_im, c_re, c_im, d_skip):
    eye = jnp.eye(S5_GB, dtype=bool)

    def bd(b):
        b4 = b.reshape(S5_NB, S5_GB, SSM_STATE, SSM_GROUP).transpose(0, 1, 3, 2)
        m = jnp.where(eye[None, :, None, :, None], b4[:, :, :, None, :], 0.0)
        return m.reshape(S5_NB, S5_CH, S5_ST)

    def cd(c):
        c4 = c.reshape(S5_NB, S5_GB, SSM_GROUP, SSM_STATE).transpose(0, 1, 3, 2)
        m = jnp.where(eye[None, :, None, :, None], c4[:, :, :, None, :], 0.0)
        return m.reshape(S5_NB, S5_ST, S5_CH)

    return (lam_re.reshape(S5_NB, 1, S5_ST), lam_im.reshape(S5_NB, 1, S5_ST),
            jnp.repeat(log_dt, SSM_STATE).reshape(S5_NB, 1, S5_ST),
            bd(b_re), bd(b_im), cd(c_re), cd(c_im), d_skip.reshape(S5_NB, 1, S5_CH))


def _ssm_glu_kernel(ap_ref, as_ref, w_ref, sp_ref, ss_ref, o_ref):
    i = pl.program_id(0)
    wb = w_ref[...].astype(BF16)

    @pl.when(i < NPT)
    def _():
        z = _dot(ap_ref[...].astype(BF16), wb)
        o_ref[...] = (sp_ref[...] * jax.nn.sigmoid(z)).astype(o_ref.dtype)

    @pl.when(i == NPT)
    def _():
        z = _dot(as_ref[...].astype(BF16), wb)
        o_ref[0:S_ROWS, :] = (ss_ref[...] * jax.nn.sigmoid(z)).astype(o_ref.dtype)


def ssm_glu(sg_p, sg_s, w_glu):
    tn = 512
    last = NPT - 1
    return pl.pallas_call(
        _ssm_glu_kernel,
        out_shape=jax.ShapeDtypeStruct((N_ROWS, SSM_WIDTH), BF16),
        grid=(NPT + 1, SSM_WIDTH // tn),
        in_specs=[
            pl.BlockSpec((TM, SSM_WIDTH), lambda i, j: (jnp.minimum(i, last), 0)),
            pl.BlockSpec((S_ROWS, SSM_WIDTH), lambda i, j: (0, 0)),
            pl.BlockSpec((SSM_WIDTH, tn), lambda i, j: (0, j)),
            pl.BlockSpec((TM, tn), lambda i, j: (jnp.minimum(i, last), j)),
            pl.BlockSpec((S_ROWS, tn), lambda i, j: (0, j)),
        ],
        out_specs=pl.BlockSpec((TM, tn), lambda i, j: (i, j)),
        compiler_params=_params(2),
        name="ssm_glu",
    )(sg_p, sg_s, w_glu, sg_p, sg_s)


def _merge_kernel(cp_ref, cs_ref, y_ref, wa_ref, wb_ref, ga_ref, gb_ref, o_ref):
    i = pl.program_id(0)
    wa = wa_ref[...].astype(BF16)
    wb = wb_ref[...].astype(BF16)

    def body(c, rows):
        ya = _dot(c, wa)
        yb = _dot(y_ref[0:rows, :], wb)
        o_ref[0:rows, :] = (ga_ref[0:rows, :] * ya + gb_ref[0:rows, :] * yb).astype(o_ref.dtype)

    @pl.when(i < NPT)
    def _():
        body(cp_ref[...], TM)

    @pl.when(i == NPT)
    def _():
        body(cs_ref[...], S_ROWS)


def merge(c_p, c_s, yg, w_conv_out, w_ssm_out, gates):
    tn = 512
    nj = D_MODEL // tn
    last = NPT - 1
    return pl.pallas_call(
        _merge_kernel,
        out_shape=jax.ShapeDtypeStruct((N_ROWS, D_MODEL), BF16),
        grid=(NPT + 1, nj),
        in_specs=[
            pl.BlockSpec((TM, CONV_WIDTH), lambda i, j: (jnp.minimum(i, last), 0)),
            pl.BlockSpec((S_ROWS, CONV_WIDTH), lambda i, j: (0, 0)),
            pl.BlockSpec((TM, SSM_WIDTH), lambda i, j: (i, 0)),
            pl.BlockSpec((CONV_WIDTH, tn), lambda i, j: (0, j)),
            pl.BlockSpec((SSM_WIDTH, tn), lambda i, j: (0, j)),
            pl.BlockSpec((TM, tn), lambda i, j: (i, j)),
            pl.BlockSpec((TM, tn), lambda i, j: (i, j + nj)),
        ],
        out_specs=pl.BlockSpec((TM, tn), lambda i, j: (i, j)),
        compiler_params=_params(2),
        name="merge",
    )(c_p, c_s, yg, w_conv_out, w_ssm_out, gates, gates)


def _oproj_kernel(a_ref, w_ref, xp_ref, xs_ref, o_ref):
    i = pl.program_id(0)
    wb = w_ref[...].astype(BF16)

    @pl.when(i < NPT)
    def _():
        o_ref[...] = xp_ref[...] + _dot(a_ref[...], wb)

    @pl.when(i == NPT)
    def _():
        o_ref[0:S_ROWS, :] = xs_ref[...] + _dot(a_ref[0:S_ROWS, :], wb)


def oproj(merged, w_o, xp, xs):
    tn = 512
    last = NPT - 1
    return pl.pallas_call(
        _oproj_kernel,
        out_shape=jax.ShapeDtypeStruct((N_ROWS, D_MODEL), F32),
        grid=(NPT + 1, D_MODEL // tn),
        in_specs=[
            pl.BlockSpec((TM, D_MODEL), lambda i, j: (i, 0)),
            pl.BlockSpec((D_MODEL, tn), lambda i, j: (0, j)),
            pl.BlockSpec((TM, tn), lambda i, j: (jnp.minimum(i, last), j)),
            pl.BlockSpec((S_ROWS, tn), lambda i, j: (0, j)),
        ],
        out_specs=pl.BlockSpec((TM, tn), lambda i, j: (i, j)),
        compiler_params=_params(2),
        name="oproj",
    )(merged, w_o, xp, xs)


FFN_TN = 256
FFN_NJ = D_FF // FFN_TN
FFN_PAD = V7X_SUBLANES


def _ffn_up_kernel(a_ref, wg_ref, wv_ref, s0_ref, s1_ref, cw_ref, cb_ref,
                   h_ref, tail_ref, gs_ref, gbuf_ref, carry_ref):
    i = pl.program_id(0)
    j = pl.program_id(1)
    wg = wg_ref[...].astype(BF16)
    wv = wv_ref[...].astype(BF16)
    w0, w1, w2 = cw_ref[0:1, :], cw_ref[1:2, :], cw_ref[2:3, :]
    cb = cb_ref[...]

    @pl.when(i < NPT)
    def _():
        a = a_ref[...]
        gate = _dot(a, wg)
        val = _dot(a, wv)
        seq_start = (i % (SEQ // TM)) == 0

        @pl.when(seq_start)
        def _():
            gbuf_ref[0:FFN_PAD, :] = jnp.zeros((FFN_PAD, FFN_TN), F32)

        @pl.when(jnp.logical_not(seq_start))
        def _():
            gbuf_ref[0:FFN_PAD, :] = carry_ref[j]

        gbuf_ref[FFN_PAD:FFN_PAD + TM, :] = gate
        gm1 = gbuf_ref[FFN_PAD - 1:FFN_PAD - 1 + TM, :]
        gm2 = gbuf_ref[FFN_PAD - 2:FFN_PAD - 2 + TM, :]
        gc = w0 * gm2 + w1 * gm1 + w2 * gate + cb
        h_ref[...] = (gc * jax.nn.sigmoid(gc) * val).astype(h_ref.dtype)
        carry_ref[j] = gate[TM - FFN_PAD:TM, :]
        tail_ref[0] = gate[TM - FFN_PAD:TM, :]

    @pl.when(i == NPT)
    def _():
        a = a_ref[0:S_ROWS, :]
        gate = _dot(a, wg)
        val = _dot(a, wv)
        gc = w0 * s0_ref[...] + w1 * s1_ref[...] + w2 * gate + cb
        h_ref[0:S_ROWS, :] = (gc * jax.nn.sigmoid(gc) * val).astype(h_ref.dtype)
        gs_ref[...] = gate
        tail_ref[0] = carry_ref[j]


def ffn_up(xn2, w_up, ffn_state2d, ffn_conv_w, ffn_conv_b):
    last = NPT - 1
    nj = FFN_NJ
    return pl.pallas_call(
        _ffn_up_kernel,
        out_shape=(jax.ShapeDtypeStruct((N_ROWS, D_FF), BF16),
                   jax.ShapeDtypeStruct((NPT, FFN_PAD, D_FF), F32),
                   jax.ShapeDtypeStruct((S_ROWS, D_FF), F32)),
        grid=(NPT + 1, nj),
        in_specs=[
            pl.BlockSpec((TM, D_MODEL), lambda i, j: (i, 0)),
            pl.BlockSpec((D_MODEL, FFN_TN), lambda i, j: (0, j)),
            pl.BlockSpec((D_MODEL, FFN_TN), lambda i, j: (0, j + nj)),
            pl.BlockSpec((S_ROWS, FFN_TN), lambda i, j: (0, j)),
            pl.BlockSpec((S_ROWS, FFN_TN), lambda i, j: (0, j + nj)),
            pl.BlockSpec((FFN_K, FFN_TN), lambda i, j: (0, j)),
            pl.BlockSpec((1, FFN_TN), lambda i, j: (0, j)),
        ],
        out_specs=(
            pl.BlockSpec((TM, FFN_TN), lambda i, j: (i, j)),
            pl.BlockSpec((1, FFN_PAD, FFN_TN), lambda i, j: (jnp.minimum(i, last), 0, j)),
            pl.BlockSpec((S_ROWS, FFN_TN), lambda i, j: (0, jnp.where(i == NPT, j, 0))),
        ),
        scratch_shapes=[
            pltpu.VMEM((FFN_PAD + TM, FFN_TN), F32),
            pltpu.VMEM((nj, FFN_PAD, FFN_TN), F32),
        ],
        compiler_params=_params(2),
        name="ffn_up",
    )(xn2, w_up, w_up, ffn_state2d, ffn_state2d, ffn_conv_w, ffn_conv_b)


def _cast_kernel(x_ref, o_ref):
    o_ref[...] = x_ref[...].astype(o_ref.dtype)


def cast_bf16(w, row_block):
    r, c = w.shape
    return pl.pallas_call(
        _cast_kernel,
        out_shape=jax.ShapeDtypeStruct((r, c), BF16),
        grid=(r // row_block,),
        in_specs=[pl.BlockSpec((row_block, c), lambda i: (i, 0))],
        out_specs=pl.BlockSpec((row_block, c), lambda i: (i, 0)),
        compiler_params=_params(1),
        name="cast_bf16",
    )(w)


DOWN_TM = 512
DOWN_NPT = P_ROWS // DOWN_TM


def _ffn_down_kernel(a_ref, w_ref, x_ref, o_ref):
    i = pl.program_id(0)

    @pl.when(i < DOWN_NPT)
    def _():
        o_ref[...] = x_ref[...] + _dot(a_ref[...], w_ref[...])

    @pl.when(i == DOWN_NPT)
    def _():
        o_ref[0:S_ROWS, :] = x_ref[0:S_ROWS, :] + _dot(a_ref[0:S_ROWS, :], w_ref[...])


def ffn_down(h, w_down_bf16, x1):
    tn = 512
    return pl.pallas_call(
        _ffn_down_kernel,
        out_shape=jax.ShapeDtypeStruct((N_ROWS, D_MODEL), F32),
        grid=(DOWN_NPT + 1, D_MODEL // tn),
        in_specs=[
            pl.BlockSpec((DOWN_TM, D_FF), lambda i, j: (i, 0)),
            pl.BlockSpec((D_FF, tn), lambda i, j: (0, j)),
            pl.BlockSpec((DOWN_TM, tn), lambda i, j: (i, j)),
        ],
        out_specs=pl.BlockSpec((DOWN_TM, tn), lambda i, j: (i, j)),
        compiler_params=_params(2),
        name="ffn_down",
    )(h, w_down_bf16, x1)


def kernel(x_prompt, x_sample, state_conv, state_ssm_re, state_ssm_im, state_ffn_conv,
           norm_mix_g, w_in, conv_w, conv_b, ln_g, ln_b, w_conv_out,
           lam_re, lam_im, log_dt, b_re, b_im, c_re, c_im, d_skip, w_glu, w_ssm_out, w_o,
           norm_ffn_g, w_up, ffn_conv_w, ffn_conv_b, w_down, final_norm_g):
    xp = x_prompt.reshape(P_ROWS, D_MODEL)
    xs = x_sample.reshape(S_ROWS, D_MODEL)

    def row(v):
        return v.reshape(1, -1)

    xn = prenorm(xp, xs, row(norm_mix_g[0]))
    glu = inproj_glu(xn, w_in[0])
    u = inproj_cols(xn, w_in[0], 2 * CONV_WIDTH, SSM_WIDTH, False, "inproj_ssm")
    gates = inproj_cols(xn, w_in[0], 2 * CONV_WIDTH + SSM_WIDTH, 2 * D_MODEL, True, "inproj_gates")

    conv_vecs = (conv_w[0], row(conv_b[0]), row(ln_g[0]), row(ln_b[0]))
    c_p, conv_p = conv_prompt(glu, *conv_vecs)
    c_s, conv_s = conv_sample(glu, state_conv[0].reshape(S_ROWS, (CONV_K - 1) * CONV_WIDTH), *conv_vecs)
    conv_s = conv_s.reshape(S_ROWS, CONV_K - 1, CONV_WIDTH)

    s5p = s5_block_params(lam_re[0], lam_im[0], log_dt[0], b_re[0], b_im[0], c_re[0], c_im[0], d_skip[0])
    n_state = SSM_GROUPS * SSM_STATE
    sg_p, ssr_p, ssi_p = s5_prompt(u, s5p)
    sg_s, ssr_s, ssi_s = s5_sample(u, state_ssm_re[0].reshape(S_ROWS, n_state),
                                   state_ssm_im[0].reshape(S_ROWS, n_state), s5p)
    yg = ssm_glu(sg_p, sg_s, w_glu[0])

    merged = merge(c_p, c_s, yg, w_conv_out[0], w_ssm_out[0], gates)
    x1 = oproj(merged, w_o[0], xp, xs)

    xn2 = rownorm(x1, row(norm_ffn_g[0]), BF16, N_ROWS, 640)
    ffn_state2d = state_ffn_conv[0].reshape(S_ROWS, (FFN_K - 1) * D_FF)
    h, gate_tail, gate_s = ffn_up(xn2, w_up[0], ffn_state2d, ffn_conv_w[0], row(ffn_conv_b[0]))
    x2 = ffn_down(h, cast_bf16(w_down[0], 688), x1)

    fg = row(final_norm_g)
    y_p = rownorm(x2, fg, F32, P_ROWS, 512)
    y_s = rownorm(x2, fg, F32, S_ROWS, S_ROWS, first_block=S_BLK)

    tiles_per_seq = SEQ // TM
    ffn_p = gate_tail[tiles_per_seq - 1::tiles_per_seq, FFN_PAD - (FFN_K - 1):, :]
    ffn_s = jnp.stack([state_ffn_conv[0, :, 1, :], gate_s], axis=1)
    state_shape = (1, -1, SSM_GROUPS, SSM_STATE)
    return (y_p.reshape(BATCH, SEQ, D_MODEL), y_s.reshape(DEC_BATCH, 1, D_MODEL),
            conv_p[None], conv_s[None],
            ssr_p.reshape(state_shape), ssi_p.reshape(state_shape),
            ssr_s.reshape(state_shape), ssi_s.reshape(state_shape),
            ffn_p[None], ffn_s[None])
```

```python
import functools
import math

import jax
import jax.numpy as jnp
from jax import lax
from jax.experimental import pallas as pl
from jax.experimental.pallas import tpu as pltpu

D_MODEL = 4096
BATCH = 4
SEQ = 2048
DEC_BATCH = 128
CONV_WIDTH = D_MODEL // 2
CONV_K = 31
SSM_WIDTH = D_MODEL // 2
SSM_GROUP = 16
SSM_GROUPS = SSM_WIDTH // SSM_GROUP
SSM_STATE = 64
D_FF = 11008
FFN_K = 3
EPS = 1e-6

P_ROWS = BATCH * SEQ
S_ROWS = DEC_BATCH

V7X_SUBLANES = 8
V7X_LANES = 128
V7X_VMEM_LIMIT = 56 * 1024 * 1024

TM = SEQ
NPT = P_ROWS // TM
NSUB = 4
SUB = TM // NSUB

BF16 = jnp.bfloat16
F32 = jnp.float32


def _params(n_grid_dims):
    return pltpu.CompilerParams(
        dimension_semantics=("arbitrary",) * n_grid_dims,
        vmem_limit_bytes=V7X_VMEM_LIMIT)


def _dot(a, b):
    return jnp.dot(a, b, preferred_element_type=F32)


def _rms(x, g):
    return x * lax.rsqrt(jnp.mean(x * x, axis=-1, keepdims=True) + EPS) * g


def _interleave(n, mm, epi):
    mm(0)
    for k in range(1, n):
        mm(k)
        epi(k - 1)
    epi(n - 1)


def _sub_rows(k):
    return slice(k * SUB, (k + 1) * SUB)


def _resident(shape, index_map):
    return pl.BlockSpec(shape, index_map, pipeline_mode=pl.Buffered(1))


def _sample_cols(i, j):
    return (0, jnp.where(i == NPT - 1, j, 0))


def _on_last_row_tile(fn):
    pl.when(pl.program_id(0) == NPT - 1)(fn)


def _rownorm_kernel(x_ref, g_ref, o_ref):
    o_ref[...] = _rms(x_ref[...], g_ref[...]).astype(o_ref.dtype)


def rownorm(x, g, out_dtype, row_block):
    rows = x.shape[0]
    return pl.pallas_call(
        _rownorm_kernel,
        out_shape=jax.ShapeDtypeStruct((rows, D_MODEL), out_dtype),
        grid=(rows // row_block,),
        in_specs=[
            pl.BlockSpec((row_block, D_MODEL), lambda i: (i, 0)),
            pl.BlockSpec((1, D_MODEL), lambda i: (0, 0)),
        ],
        out_specs=pl.BlockSpec((row_block, D_MODEL), lambda i: (i, 0)),
        compiler_params=_params(1),
        name="rownorm",
    )(x, g)


def rownorm_pair(xp, xs, g, out_dtype):
    return rownorm(xp, g, out_dtype, 512), rownorm(xs, g, out_dtype, S_ROWS)


GLU_TN = 256


def _inproj_glu_kernel(ap_ref, as_ref, wa_ref, wg_ref, op_ref, os_ref, wc_ref, r_ref):
    tn = GLU_TN
    wc_ref[:, 0:tn] = wa_ref[...].astype(BF16)
    wc_ref[:, tn:2 * tn] = wg_ref[...].astype(BF16)

    def mm(k):
        r_ref[_sub_rows(k), :] = _dot(ap_ref[_sub_rows(k), :], wc_ref[...])

    def epi(k):
        r = r_ref[_sub_rows(k), :]
        op_ref[_sub_rows(k), :] = r[:, 0:tn] * jax.nn.sigmoid(r[:, tn:2 * tn])

    _interleave(NSUB, mm, epi)

    @_on_last_row_tile
    def _():
        r = _dot(as_ref[...], wc_ref[...])
        os_ref[...] = r[:, 0:tn] * jax.nn.sigmoid(r[:, tn:2 * tn])


def inproj_glu(xn_p, xn_s, w_in):
    tn = GLU_TN
    nj = CONV_WIDTH // tn
    return pl.pallas_call(
        _inproj_glu_kernel,
        out_shape=(jax.ShapeDtypeStruct((P_ROWS, CONV_WIDTH), F32),
                   jax.ShapeDtypeStruct((S_ROWS, CONV_WIDTH), F32)),
        grid=(NPT, nj),
        in_specs=[
            _resident((TM, D_MODEL), lambda i, j: (i, 0)),
            _resident((S_ROWS, D_MODEL), lambda i, j: (0, 0)),
            pl.BlockSpec((D_MODEL, tn), lambda i, j: (0, j)),
            pl.BlockSpec((D_MODEL, tn), lambda i, j: (0, j + nj)),
        ],
        out_specs=(pl.BlockSpec((TM, tn), lambda i, j: (i, j)),
                   pl.BlockSpec((S_ROWS, tn), _sample_cols)),
        scratch_shapes=[pltpu.VMEM((D_MODEL, 2 * tn), BF16), pltpu.VMEM((TM, 2 * tn), F32)],
        compiler_params=_params(2),
        name="inproj_glu",
    )(xn_p, xn_s, w_in, w_in)


def _inproj_cols_kernel(ap_ref, as_ref, w_ref, op_ref, os_ref, wc_ref, *, squash):
    wc_ref[...] = w_ref[...].astype(BF16)

    def mm(k):
        op_ref[_sub_rows(k), :] = _dot(ap_ref[_sub_rows(k), :], wc_ref[...])

    def epi(k):
        if squash:
            op_ref[_sub_rows(k), :] = jax.nn.sigmoid(op_ref[_sub_rows(k), :])

    _interleave(NSUB, mm, epi)

    @_on_last_row_tile
    def _():
        r = _dot(as_ref[...], wc_ref[...])
        os_ref[...] = jax.nn.sigmoid(r) if squash else r


def inproj_cols(xn_p, xn_s, w_in, col0, width, squash, name):
    tn = 512
    return pl.pallas_call(
        functools.partial(_inproj_cols_kernel, squash=squash),
        out_shape=(jax.ShapeDtypeStruct((P_ROWS, width), F32),
                   jax.ShapeDtypeStruct((S_ROWS, width), F32)),
        grid=(NPT, width // tn),
        in_specs=[
            _resident((TM, D_MODEL), lambda i, j: (i, 0)),
            _resident((S_ROWS, D_MODEL), lambda i, j: (0, 0)),
            pl.BlockSpec((D_MODEL, tn), lambda i, j: (0, j + col0 // tn)),
        ],
        out_specs=(pl.BlockSpec((TM, tn), lambda i, j: (i, j)),
                   pl.BlockSpec((S_ROWS, tn), _sample_cols)),
        scratch_shapes=[pltpu.VMEM((D_MODEL, tn), BF16)],
        compiler_params=_params(2),
        name=name,
    )(xn_p, xn_s, w_in)


CONV_TT = 256
CONV_HALO = 32
CONV_RC = 32
CONV_LC = 256
CONV_PIECE = 32


def _ln_swish(c, g, b):
    mu = jnp.mean(c, axis=-1, keepdims=True)
    d = c - mu
    var = jnp.mean(d * d, axis=-1, keepdims=True)
    r = d * lax.rsqrt(var + EPS) * g + b
    return r * jax.nn.sigmoid(r)


def _conv_prompt_kernel(x_ref, w_ref, cb_ref, g_ref, b_ref, o_ref, st_ref, xs_ref, cbuf_ref):
    t = pl.program_id(1)
    nt = pl.num_programs(1)
    sub = V7X_SUBLANES

    @pl.when(t == 0)
    def _():
        xs_ref[0, 0:CONV_HALO, :] = jnp.zeros((CONV_HALO, CONV_WIDTH), F32)
        xs_ref[0, CONV_HALO + CONV_TT:CONV_HALO + CONV_TT + sub, :] = jnp.zeros((sub, CONV_WIDTH), F32)

    xs_ref[0, CONV_HALO:CONV_HALO + CONV_TT, :] = x_ref[...]

    def shift_piece(p, carry):
        r = pl.multiple_of(p * CONV_PIECE, CONV_PIECE)
        piece = xs_ref[0, pl.ds(r, CONV_PIECE + sub), :]
        for m in range(1, sub):
            rolled = pltpu.roll(piece, CONV_PIECE + sub - m, axis=0)
            xs_ref[m, pl.ds(r, CONV_PIECE), :] = rolled[0:CONV_PIECE]
        return carry

    lax.fori_loop(0, (CONV_HALO + CONV_TT) // CONV_PIECE, shift_piece, 0)

    off = CONV_HALO - (CONV_K - 1)

    def chunk(c, carry):
        r0 = pl.multiple_of(c * CONV_RC, CONV_RC)
        for l0 in range(0, CONV_WIDTH, CONV_LC):
            lanes = slice(l0, l0 + CONV_LC)
            acc = jnp.zeros((CONV_RC, CONV_LC), F32)
            for k in range(CONV_K):
                o = off + k
                win = xs_ref[o % sub, pl.ds(r0 + (o // sub) * sub, CONV_RC), lanes]
                acc = acc + win * w_ref[k:k + 1, lanes]
            cbuf_ref[:, lanes] = acc + cb_ref[:, lanes]
        o_ref[pl.ds(r0, CONV_RC), :] = _ln_swish(cbuf_ref[...], g_ref[...], b_ref[...]).astype(o_ref.dtype)
        return carry

    lax.fori_loop(0, CONV_TT // CONV_RC, chunk, 0)

    @pl.when(t == nt - 1)
    def _():
        st_ref[0] = xs_ref[off, CONV_TT:CONV_TT + CONV_K - 1, :]

    xs_ref[0, 0:CONV_HALO, :] = xs_ref[0, CONV_TT:CONV_TT + CONV_HALO, :]


def conv_prompt(glu, conv_w, conv_b, ln_g, ln_b):
    nt = SEQ // CONV_TT
    vec = pl.BlockSpec((1, CONV_WIDTH), lambda b, t: (0, 0))
    return pl.pallas_call(
        _conv_prompt_kernel,
        out_shape=(jax.ShapeDtypeStruct((P_ROWS, CONV_WIDTH), BF16),
                   jax.ShapeDtypeStruct((BATCH, CONV_K - 1, CONV_WIDTH), F32)),
        grid=(BATCH, nt),
        in_specs=[
            pl.BlockSpec((CONV_TT, CONV_WIDTH), lambda b, t: (b * nt + t, 0)),
            pl.BlockSpec((CONV_K, CONV_WIDTH), lambda b, t: (0, 0)),
            vec, vec, vec,
        ],
        out_specs=(pl.BlockSpec((CONV_TT, CONV_WIDTH), lambda b, t: (b * nt + t, 0)),
                   pl.BlockSpec((1, CONV_K - 1, CONV_WIDTH), lambda b, t: (b, 0, 0))),
        scratch_shapes=[
            pltpu.VMEM((V7X_SUBLANES, CONV_HALO + CONV_TT + V7X_SUBLANES, CONV_WIDTH), F32),
            pltpu.VMEM((CONV_RC, CONV_WIDTH), F32),
        ],
        compiler_params=_params(2),
        name="conv_prompt",
    )(glu, conv_w, conv_b, ln_g, ln_b)


CONV_SB = 16


def _conv_sample_kernel(x_ref, st_ref, w_ref, cb_ref, g_ref, b_ref, o_ref, nst_ref, cbuf_ref):
    nb = CONV_K - 1
    w_old = w_ref[0:nb, :]
    w_new = w_ref[nb:CONV_K, :]
    for s in range(CONV_SB):
        st = st_ref[s]
        x = x_ref[s:s + 1, :]
        cbuf_ref[s:s + 1, :] = jnp.sum(st * w_old, axis=0, keepdims=True) + x * w_new
        nst_ref[s, 0:nb - 1, :] = st_ref[s, 1:nb, :]
        nst_ref[s, nb - 1:nb, :] = x
    o_ref[...] = _ln_swish(cbuf_ref[...] + cb_ref[...], g_ref[...], b_ref[...]).astype(o_ref.dtype)


def conv_sample(glu, state_conv, conv_w, conv_b, ln_g, ln_b):
    vec = pl.BlockSpec((1, CONV_WIDTH), lambda i: (0, 0))
    st_spec = pl.BlockSpec((CONV_SB, CONV_K - 1, CONV_WIDTH), lambda i: (i, 0, 0))
    return pl.pallas_call(
        _conv_sample_kernel,
        out_shape=(jax.ShapeDtypeStruct((S_ROWS, CONV_WIDTH), BF16),
                   jax.ShapeDtypeStruct((S_ROWS, CONV_K - 1, CONV_WIDTH), F32)),
        grid=(S_ROWS // CONV_SB,),
        in_specs=[
            pl.BlockSpec((CONV_SB, CONV_WIDTH), lambda i: (i, 0)),
            st_spec,
            pl.BlockSpec((CONV_K, CONV_WIDTH), lambda i: (0, 0)),
            vec, vec, vec,
        ],
        out_specs=(pl.BlockSpec((CONV_SB, CONV_WIDTH), lambda i: (i, 0)), st_spec),
        scratch_shapes=[pltpu.VMEM((CONV_SB, CONV_WIDTH), F32)],
        compiler_params=_params(1),
        name="conv_sample",
    )(glu, state_conv, conv_w, conv_b, ln_g, ln_b)


S5_GB = 16
S5_CH = S5_GB * SSM_GROUP
S5_ST = S5_GB * SSM_STATE
S5_NB = SSM_GROUPS // S5_GB
S5_SEG = V7X_SUBLANES
S5_SEGLEN = SEQ // S5_SEG
S5_TC = 64
S5_NC = S5_SEGLEN // S5_TC
S5_CR = S5_TC * S5_SEG
S5_LH = S5_CH // V7X_LANES


def _s5_discretize(lr_ref, li_ref, ldt_ref):
    lr, li = lr_ref[0], li_ref[0]
    dt = jnp.exp(ldt_ref[0])
    mag = jnp.exp(lr * dt)
    a_re, a_im = mag * jnp.cos(li * dt), mag * jnp.sin(li * dt)
    den = lr * lr + li * li
    nr, ni = a_re - 1.0, a_im
    f_re = (nr * lr + ni * li) / den
    f_im = (ni * lr - nr * li) / den
    return a_re, a_im, f_re, f_im


def _s5_fill_weights(f_re, f_im, bdre_ref, bdim_ref, cdre_ref, cdim_ref, bb_ref, cc_ref):
    bdre, bdim = bdre_ref[0], bdim_ref[0]
    bb_ref[:, 0:S5_ST] = (f_re * bdre - f_im * bdim).astype(BF16)
    bb_ref[:, S5_ST:2 * S5_ST] = (f_re * bdim + f_im * bdre).astype(BF16)
    cc_ref[0:S5_ST, :] = cdre_ref[0].astype(BF16)
    cc_ref[S5_ST:2 * S5_ST, :] = (-cdim_ref[0]).astype(BF16)


def _s5_prompt_kernel(u_ref, lr_ref, li_ref, ldt_ref, bdre_ref, bdim_ref, cdre_ref, cdim_ref, d_ref,
                      sg_ref, hre_ref, him_ref,
                      bb_ref, cc_ref, ul_ref, sgl_ref, lhs_ref, bu_ref, hch_ref):
    a_re, a_im, f_re, f_im = _s5_discretize(lr_ref, li_ref, ldt_ref)
    _s5_fill_weights(f_re, f_im, bdre_ref, bdim_ref, cdre_ref, cdim_ref, bb_ref, cc_ref)
    are8 = jnp.broadcast_to(a_re, (S5_SEG, S5_ST))
    aim8 = jnp.broadcast_to(a_im, (S5_SEG, S5_ST))
    d = d_ref[0]
    for hh in range(S5_LH):
        ul_ref[hh] = u_ref[:, hh * V7X_LANES:(hh + 1) * V7X_LANES]

    def chunk_rows(c):
        return slice(c * S5_CR, (c + 1) * S5_CR)

    def project_in(c):
        for tl in range(S5_TC):
            t = c * S5_TC + tl
            for hh in range(S5_LH):
                lhs_ref[t * S5_SEG:(t + 1) * S5_SEG, hh * V7X_LANES:(hh + 1) * V7X_LANES] = (
                    ul_ref[hh, pl.ds(t, S5_SEG, stride=S5_SEGLEN), :])
        bu_ref[chunk_rows(c), :] = _dot(lhs_ref[chunk_rows(c), :].astype(BF16), bb_ref[...])

    def scan(c, hr, hi, keep):
        for tl in range(S5_TC):
            t = c * S5_TC + tl
            rows = slice(t * S5_SEG, (t + 1) * S5_SEG)
            br = bu_ref[rows, 0:S5_ST]
            bi = bu_ref[rows, S5_ST:2 * S5_ST]
            hr, hi = are8 * hr - aim8 * hi + br, are8 * hi + aim8 * hr + bi
            if keep:
                crow = slice(tl * S5_SEG, (tl + 1) * S5_SEG)
                hch_ref[c % 2, crow, 0:S5_ST] = hr
                hch_ref[c % 2, crow, S5_ST:2 * S5_ST] = hi
        return hr, hi

    def project_out(c):
        y = _dot(hch_ref[c % 2].astype(BF16), cc_ref[...]) + d * lhs_ref[chunk_rows(c), :]
        lhs_ref[chunk_rows(c), :] = jax.nn.gelu(y)
        for tl in range(S5_TC):
            t = c * S5_TC + tl
            for hh in range(S5_LH):
                sgl_ref[hh, pl.ds(t, S5_SEG, stride=S5_SEGLEN), :] = (
                    lhs_ref[t * S5_SEG:(t + 1) * S5_SEG, hh * V7X_LANES:(hh + 1) * V7X_LANES])

    zeros = jnp.zeros((S5_SEG, S5_ST), F32)
    e_re, e_im = zeros, zeros
    project_in(0)
    for c in range(S5_NC):
        if c + 1 < S5_NC:
            project_in(c + 1)
        e_re, e_im = scan(c, e_re, e_im, keep=False)

    p_re, p_im = a_re, a_im
    for _ in range(int(math.log2(S5_SEGLEN))):
        p_re, p_im = p_re * p_re - p_im * p_im, 2.0 * p_re * p_im
    seg = lax.broadcasted_iota(jnp.int32, (S5_SEG, S5_ST), 0)
    qr = qi = jnp.zeros((1, S5_ST), F32)
    h_re = h_im = zeros
    for s in range(1, S5_SEG):
        qr, qi = (p_re * qr - p_im * qi + e_re[s - 1:s, :],
                  p_re * qi + p_im * qr + e_im[s - 1:s, :])
        h_re = jnp.where(seg == s, qr, h_re)
        h_im = jnp.where(seg == s, qi, h_im)

    for c in range(S5_NC):
        h_re, h_im = scan(c, h_re, h_im, keep=True)
        if c > 0:
            project_out(c - 1)
    project_out(S5_NC - 1)

    hre_ref[0] = h_re[S5_SEG - 1:S5_SEG, :]
    him_ref[0] = h_im[S5_SEG - 1:S5_SEG, :]
    for hh in range(S5_LH):
        sg_ref[:, hh * V7X_LANES:(hh + 1) * V7X_LANES] = sgl_ref[hh]


def _s5_param_specs(idx):
    def spec(shape):
        return pl.BlockSpec((1,) + shape, lambda *g: (idx(*g), 0, 0))
    return [spec((1, S5_ST)), spec((1, S5_ST)), spec((1, S5_ST)),
            spec((S5_CH, S5_ST)), spec((S5_CH, S5_ST)),
            spec((S5_ST, S5_CH)), spec((S5_ST, S5_CH)), spec((1, S5_CH))]


def s5_prompt(u, s5p):
    return pl.pallas_call(
        _s5_prompt_kernel,
        out_shape=(jax.ShapeDtypeStruct((P_ROWS, SSM_WIDTH), F32),
                   jax.ShapeDtypeStruct((BATCH, 1, SSM_GROUPS * SSM_STATE), F32),
                   jax.ShapeDtypeStruct((BATCH, 1, SSM_GROUPS * SSM_STATE), F32)),
        grid=(S5_NB, BATCH),
        in_specs=[pl.BlockSpec((SEQ, S5_CH), lambda j, b: (b, j))] + _s5_param_specs(lambda j, b: j),
        out_specs=(pl.BlockSpec((SEQ, S5_CH), lambda j, b: (b, j)),
                   pl.BlockSpec((1, 1, S5_ST), lambda j, b: (b, 0, j)),
                   pl.BlockSpec((1, 1, S5_ST), lambda j, b: (b, 0, j))),
        scratch_shapes=[
            pltpu.VMEM((S5_CH, 2 * S5_ST), BF16),
            pltpu.VMEM((2 * S5_ST, S5_CH), BF16),
            pltpu.VMEM((S5_LH, SEQ, V7X_LANES), F32),
            pltpu.VMEM((S5_LH, SEQ, V7X_LANES), F32),
            pltpu.VMEM((SEQ, S5_CH), F32),
            pltpu.VMEM((SEQ, 2 * S5_ST), F32),
            pltpu.VMEM((2, S5_CR, 2 * S5_ST), F32),
        ],
        compiler_params=_params(2),
        name="s5_prompt",
    )(u, *s5p)


def _s5_sample_kernel(u_ref, h0re_ref, h0im_ref, lr_ref, li_ref, ldt_ref, bdre_ref, bdim_ref,
                      cdre_ref, cdim_ref, d_ref, sg_ref, hre_ref, him_ref, bb_ref, cc_ref):
    a_re, a_im, f_re, f_im = _s5_discretize(lr_ref, li_ref, ldt_ref)
    _s5_fill_weights(f_re, f_im, bdre_ref, bdim_ref, cdre_ref, cdim_ref, bb_ref, cc_ref)
    u = u_ref[...]
    bu = _dot(u.astype(BF16), bb_ref[...])
    h0r, h0i = h0re_ref[...], h0im_ref[...]
    hr = a_re * h0r - a_im * h0i + bu[:, 0:S5_ST]
    hi = a_re * h0i + a_im * h0r + bu[:, S5_ST:2 * S5_ST]
    hre_ref[...] = hr
    him_ref[...] = hi
    y = _dot(hr.astype(BF16), cc_ref[0:S5_ST, :]) + _dot(hi.astype(BF16), cc_ref[S5_ST:2 * S5_ST, :])
    sg_ref[...] = jax.nn.gelu(y + d_ref[0] * u)


def s5_sample(u, h0_re, h0_im, s5p):
    st = pl.BlockSpec((S_ROWS, S5_ST), lambda j: (0, j))
    return pl.pallas_call(
        _s5_sample_kernel,
        out_shape=(jax.ShapeDtypeStruct((S_ROWS, SSM_WIDTH), F32),
                   jax.ShapeDtypeStruct((S_ROWS, SSM_GROUPS * SSM_STATE), F32),
                   jax.ShapeDtypeStruct((S_ROWS, SSM_GROUPS * SSM_STATE), F32)),
        grid=(S5_NB,),
        in_specs=[pl.BlockSpec((S_ROWS, S5_CH), lambda j: (0, j)), st, st]
        + _s5_param_specs(lambda j: j),
        out_specs=(pl.BlockSpec((S_ROWS, S5_CH), lambda j: (0, j)), st, st),
        scratch_shapes=[
            pltpu.VMEM((S5_CH, 2 * S5_ST), BF16),
            pltpu.VMEM((2 * S5_ST, S5_CH), BF16),
        ],
        compiler_params=_params(1),
        name="s5_sample",
    )(u, h0_re, h0_im, *s5p)


def s5_block_params(lam_re, lam_im, log_dt, b_re, b_im, c_re, c_im, d_skip):
    eye = jnp.eye(S5_GB, dtype=bool)

    def bd(b):
        b4 = b.reshape(S5_NB, S5_GB, SSM_STATE, SSM_GROUP).transpose(0, 1, 3, 2)
        m = jnp.where(eye[None, :, None, :, None], b4[:, :, :, None, :], 0.0)
        return m.reshape(S5_NB, S5_CH, S5_ST)

    def cd(c):
        c4 = c.reshape(S5_NB, S5_GB, SSM_GROUP, SSM_STATE).transpose(0, 1, 3, 2)
        m = jnp.where(eye[None, :, None, :, None], c4[:, :, :, None, :], 0.0)
        return m.reshape(S5_NB, S5_ST, S5_CH)

    return (lam_re.reshape(S5_NB, 1, S5_ST), lam_im.reshape(S5_NB, 1, S5_ST),
            jnp.repeat(log_dt, SSM_STATE).reshape(S5_NB, 1, S5_ST),
            bd(b_re), bd(b_im), cd(c_re), cd(c_im), d_skip.reshape(S5_NB, 1, S5_CH))


def _ssm_glu_kernel(ap_ref, as_ref, w_ref, tp_ref, ts_ref, op_ref, os_ref, ab_ref, wc_ref, r_ref):
    @pl.when(pl.program_id(1) == 0)
    def _():
        ab_ref[...] = ap_ref[...].astype(BF16)

    wc_ref[...] = w_ref[...].astype(BF16)

    def mm(k):
        r_ref[_sub_rows(k), :] = _dot(ab_ref[_sub_rows(k), :], wc_ref[...])

    def epi(k):
        op_ref[_sub_rows(k), :] = (
            tp_ref[_sub_rows(k), :] * jax.nn.sigmoid(r_ref[_sub_rows(k), :])).astype(op_ref.dtype)

    _interleave(NSUB, mm, epi)

    @_on_last_row_tile
    def _():
        z = _dot(as_ref[...].astype(BF16), wc_ref[...])
        os_ref[...] = (ts_ref[...] * jax.nn.sigmoid(z)).astype(os_ref.dtype)


def ssm_glu(sg_p, sg_s, w_glu):
    tn = 512
    return pl.pallas_call(
        _ssm_glu_kernel,
        out_shape=(jax.ShapeDtypeStruct((P_ROWS, SSM_WIDTH), BF16),
                   jax.ShapeDtypeStruct((S_ROWS, SSM_WIDTH), BF16)),
        grid=(NPT, SSM_WIDTH // tn),
        in_specs=[
            _resident((TM, SSM_WIDTH), lambda i, j: (i, 0)),
            _resident((S_ROWS, SSM_WIDTH), lambda i, j: (0, 0)),
            pl.BlockSpec((SSM_WIDTH, tn), lambda i, j: (0, j)),
            pl.BlockSpec((TM, tn), lambda i, j: (i, j)),
            pl.BlockSpec((S_ROWS, tn), lambda i, j: (0, j)),
        ],
        out_specs=(pl.BlockSpec((TM, tn), lambda i, j: (i, j)),
                   pl.BlockSpec((S_ROWS, tn), _sample_cols)),
        scratch_shapes=[pltpu.VMEM((TM, SSM_WIDTH), BF16), pltpu.VMEM((SSM_WIDTH, tn), BF16),
                        pltpu.VMEM((TM, tn), F32)],
        compiler_params=_params(2),
        name="ssm_glu",
    )(sg_p, sg_s, w_glu, sg_p, sg_s)


MERGE_TN = 256


def _merge_kernel(cp_ref, cs_ref, yp_ref, ys_ref, wa_ref, wb_ref, gap_ref, gbp_ref, gas_ref, gbs_ref,
                  op_ref, os_ref, wc_ref, r_ref):
    wc_ref[0] = wa_ref[...].astype(BF16)
    wc_ref[1] = wb_ref[...].astype(BF16)

    def mm(k):
        r_ref[0, _sub_rows(k), :] = _dot(cp_ref[_sub_rows(k), :], wc_ref[0])
        r_ref[1, _sub_rows(k), :] = _dot(yp_ref[_sub_rows(k), :], wc_ref[1])

    def epi(k):
        rows = _sub_rows(k)
        op_ref[rows, :] = (gap_ref[rows, :] * r_ref[0, rows, :]
                           + gbp_ref[rows, :] * r_ref[1, rows, :]).astype(op_ref.dtype)

    _interleave(NSUB, mm, epi)

    @_on_last_row_tile
    def _():
        ya = _dot(cs_ref[...], wc_ref[0])
        yb = _dot(ys_ref[...], wc_ref[1])
        os_ref[...] = (gas_ref[...] * ya + gbs_ref[...] * yb).astype(os_ref.dtype)


def merge(c_p, c_s, yg_p, yg_s, w_conv_out, w_ssm_out, gates_p, gates_s):
    tn = MERGE_TN
    nj = D_MODEL // tn
    return pl.pallas_call(
        _merge_kernel,
        out_shape=(jax.ShapeDtypeStruct((P_ROWS, D_MODEL), BF16),
                   jax.ShapeDtypeStruct((S_ROWS, D_MODEL), BF16)),
        grid=(NPT, nj),
        in_specs=[
            _resident((TM, CONV_WIDTH), lambda i, j: (i, 0)),
            _resident((S_ROWS, CONV_WIDTH), lambda i, j: (0, 0)),
            _resident((TM, SSM_WIDTH), lambda i, j: (i, 0)),
            _resident((S_ROWS, SSM_WIDTH), lambda i, j: (0, 0)),
            pl.BlockSpec((CONV_WIDTH, tn), lambda i, j: (0, j)),
            pl.BlockSpec((SSM_WIDTH, tn), lambda i, j: (0, j)),
            pl.BlockSpec((TM, tn), lambda i, j: (i, j)),
            pl.BlockSpec((TM, tn), lambda i, j: (i, j + nj)),
            pl.BlockSpec((S_ROWS, tn), lambda i, j: (0, j)),
            pl.BlockSpec((S_ROWS, tn), lambda i, j: (0, j + nj)),
        ],
        out_specs=(pl.BlockSpec((TM, tn), lambda i, j: (i, j)),
                   pl.BlockSpec((S_ROWS, tn), _sample_cols)),
        scratch_shapes=[pltpu.VMEM((2, CONV_WIDTH, tn), BF16), pltpu.VMEM((2, TM, tn), F32)],
        compiler_params=_params(2),
        name="merge",
    )(c_p, c_s, yg_p, yg_s, w_conv_out, w_ssm_out, gates_p, gates_p, gates_s, gates_s)


def _resproj_kernel(ap_ref, as_ref, w_ref, xp_ref, xs_ref, op_ref, os_ref, wc_ref, *, nsub, last):
    if wc_ref is None:
        w = w_ref
    else:
        wc_ref[...] = w_ref[...].astype(BF16)
        w = wc_ref
    sub = ap_ref.shape[0] // nsub

    def rows(k):
        return slice(k * sub, (k + 1) * sub)

    def mm(k):
        op_ref[rows(k), :] = _dot(ap_ref[rows(k), :], w[...])

    def epi(k):
        op_ref[rows(k), :] = op_ref[rows(k), :] + xp_ref[rows(k), :]

    _interleave(nsub, mm, epi)

    @pl.when(pl.program_id(0) == last)
    def _():
        os_ref[...] = xs_ref[...] + _dot(as_ref[...], w[...])


def oproj(merged_p, merged_s, w_o, xp, xs):
    tn = 512
    return pl.pallas_call(
        functools.partial(_resproj_kernel, nsub=NSUB, last=NPT - 1),
        out_shape=(jax.ShapeDtypeStruct((P_ROWS, D_MODEL), F32),
                   jax.ShapeDtypeStruct((S_ROWS, D_MODEL), F32)),
        grid=(NPT, D_MODEL // tn),
        in_specs=[
            _resident((TM, D_MODEL), lambda i, j: (i, 0)),
            _resident((S_ROWS, D_MODEL), lambda i, j: (0, 0)),
            pl.BlockSpec((D_MODEL, tn), lambda i, j: (0, j)),
            pl.BlockSpec((TM, tn), lambda i, j: (i, j)),
            pl.BlockSpec((S_ROWS, tn), lambda i, j: (0, j)),
        ],
        out_specs=(pl.BlockSpec((TM, tn), lambda i, j: (i, j)),
                   pl.BlockSpec((S_ROWS, tn), _sample_cols)),
        scratch_shapes=[pltpu.VMEM((D_MODEL, tn), BF16)],
        compiler_params=_params(2),
        name="oproj",
    )(merged_p, merged_s, w_o, xp, xs)


def _cast_kernel(x_ref, o_ref):
    o_ref[...] = x_ref[...].astype(o_ref.dtype)


def cast_bf16(w, row_block):
    r, c = w.shape
    return pl.pallas_call(
        _cast_kernel,
        out_shape=jax.ShapeDtypeStruct((r, c), BF16),
        grid=(r // row_block,),
        in_specs=[pl.BlockSpec((row_block, c), lambda i: (i, 0))],
        out_specs=pl.BlockSpec((row_block, c), lambda i: (i, 0)),
        compiler_params=_params(1),
        name="cast_bf16",
    )(w)


DOWN_TM = 512
DOWN_NPT = P_ROWS // DOWN_TM


def _ffn_down_kernel(ap_ref, as_ref, w_ref, xp_ref, xs_ref, op_ref, os_ref):
    _resproj_kernel(ap_ref, as_ref, w_ref, xp_ref, xs_ref, op_ref, os_ref, None, nsub=1, last=DOWN_NPT - 1)


def ffn_down(h_p, h_s, w_down_bf16, x1_p, x1_s):
    tn = 512
    last = DOWN_NPT - 1
    return pl.pallas_call(
        _ffn_down_kernel,
        out_shape=(jax.ShapeDtypeStruct((P_ROWS, D_MODEL), F32),
                   jax.ShapeDtypeStruct((S_ROWS, D_MODEL), F32)),
        grid=(DOWN_NPT, D_MODEL // tn),
        in_specs=[
            pl.BlockSpec((DOWN_TM, D_FF), lambda i, j: (i, 0)),
            _resident((S_ROWS, D_FF), lambda i, j: (0, 0)),
            pl.BlockSpec((D_FF, tn), lambda i, j: (0, j)),
            pl.BlockSpec((DOWN_TM, tn), lambda i, j: (i, j)),
            pl.BlockSpec((S_ROWS, tn), lambda i, j: (0, j)),
        ],
        out_specs=(pl.BlockSpec((DOWN_TM, tn), lambda i, j: (i, j)),
                   pl.BlockSpec((S_ROWS, tn), lambda i, j: (0, jnp.where(i == last, j, 0)))),
        compiler_params=_params(2),
        name="ffn_down",
    )(h_p, h_s, w_down_bf16, x1_p, x1_s)


FFN_TN = 256
FFN_NJ = D_FF // FFN_TN
FFN_PAD = V7X_SUBLANES


def _ffn_up_kernel(ap_ref, as_ref, wg_ref, wv_ref, s0_ref, s1_ref, cw_ref, cb_ref,
                   hp_ref, hs_ref, tail_ref, gs_ref, wc_ref, g_ref, v_ref):
    tn = FFN_TN
    wc_ref[:, 0:tn] = wg_ref[...].astype(BF16)
    wc_ref[:, tn:2 * tn] = wv_ref[...].astype(BF16)
    w0, w1, w2 = cw_ref[0:1, :], cw_ref[1:2, :], cw_ref[2:3, :]
    cb = cb_ref[...]
    g_ref[0:FFN_PAD, :] = jnp.zeros((FFN_PAD, tn), F32)

    def mm(k):
        r = _dot(ap_ref[_sub_rows(k), :], wc_ref[...])
        g_ref[FFN_PAD + k * SUB:FFN_PAD + (k + 1) * SUB, :] = r[:, 0:tn]
        v_ref[_sub_rows(k), :] = r[:, tn:2 * tn]

    def epi(k):
        r0 = FFN_PAD + k * SUB
        gc = (w0 * g_ref[r0 - 2:r0 - 2 + SUB, :] + w1 * g_ref[r0 - 1:r0 - 1 + SUB, :]
              + w2 * g_ref[r0:r0 + SUB, :] + cb)
        hp_ref[_sub_rows(k), :] = (gc * jax.nn.sigmoid(gc) * v_ref[_sub_rows(k), :]).astype(hp_ref.dtype)

    _interleave(NSUB, mm, epi)
    tail_ref[0] = g_ref[TM:TM + FFN_PAD, :]

    @_on_last_row_tile
    def _():
        r = _dot(as_ref[...], wc_ref[...])
        gate, val = r[:, 0:tn], r[:, tn:2 * tn]
        gc = w0 * s0_ref[...] + w1 * s1_ref[...] + w2 * gate + cb
        hs_ref[...] = (gc * jax.nn.sigmoid(gc) * val).astype(hs_ref.dtype)
        gs_ref[...] = gate


def ffn_up(xn2_p, xn2_s, w_up, ffn_old, ffn_conv_w, ffn_conv_b):
    nj = FFN_NJ
    tn = FFN_TN
    return pl.pallas_call(
        _ffn_up_kernel,
        out_shape=(jax.ShapeDtypeStruct((P_ROWS, D_FF), BF16),
                   jax.ShapeDtypeStruct((S_ROWS, D_FF), BF16),
                   jax.ShapeDtypeStruct((NPT, FFN_PAD, D_FF), F32),
                   jax.ShapeDtypeStruct((S_ROWS, D_FF), F32)),
        grid=(NPT, nj),
        in_specs=[
            _resident((TM, D_MODEL), lambda i, j: (i, 0)),
            _resident((S_ROWS, D_MODEL), lambda i, j: (0, 0)),
            pl.BlockSpec((D_MODEL, tn), lambda i, j: (0, j)),
            pl.BlockSpec((D_MODEL, tn), lambda i, j: (0, j + nj)),
            pl.BlockSpec((S_ROWS, tn), lambda i, j: (0, j)),
            pl.BlockSpec((S_ROWS, tn), lambda i, j: (0, j)),
            pl.BlockSpec((FFN_K, tn), lambda i, j: (0, j)),
            pl.BlockSpec((1, tn), lambda i, j: (0, j)),
        ],
        out_specs=(
            pl.BlockSpec((TM, tn), lambda i, j: (i, j)),
            pl.BlockSpec((S_ROWS, tn), _sample_cols),
            pl.BlockSpec((1, FFN_PAD, tn), lambda i, j: (i, 0, j)),
            pl.BlockSpec((S_ROWS, tn), _sample_cols),
        ),
        scratch_shapes=[
            pltpu.VMEM((D_MODEL, 2 * tn), BF16),
            pltpu.VMEM((FFN_PAD + TM, tn), F32),
            pltpu.VMEM((TM, tn), F32),
        ],
        compiler_params=_params(2),
        name="ffn_up",
    )(xn2_p, xn2_s, w_up, w_up, ffn_old[0], ffn_old[1], ffn_conv_w, ffn_conv_b)


def kernel(x_prompt, x_sample, state_conv, state_ssm_re, state_ssm_im, state_ffn_conv,
           norm_mix_g, w_in, conv_w, conv_b, ln_g, ln_b, w_conv_out,
           lam_re, lam_im, log_dt, b_re, b_im, c_re, c_im, d_skip, w_glu, w_ssm_out, w_o,
           norm_ffn_g, w_up, ffn_conv_w, ffn_conv_b, w_down, final_norm_g):
    xp = x_prompt.reshape(P_ROWS, D_MODEL)
    xs = x_sample.reshape(S_ROWS, D_MODEL)

    def row(v):
        return v.reshape(1, -1)

    xn_p, xn_s = rownorm_pair(xp, xs, row(norm_mix_g[0]), BF16)
    glu_p, glu_s = inproj_glu(xn_p, xn_s, w_in[0])
    u_p, u_s = inproj_cols(xn_p, xn_s, w_in[0], 2 * CONV_WIDTH, SSM_WIDTH, False, "inproj_ssm")
    gates_p, gates_s = inproj_cols(xn_p, xn_s, w_in[0], 2 * CONV_WIDTH + SSM_WIDTH, 2 * D_MODEL, True,
                                   "inproj_gates")

    conv_vecs = (conv_w[0], row(conv_b[0]), row(ln_g[0]), row(ln_b[0]))
    c_p, conv_p = conv_prompt(glu_p, *conv_vecs)
    c_s, conv_s = conv_sample(glu_s, state_conv[0], *conv_vecs)

    s5p = s5_block_params(lam_re[0], lam_im[0], log_dt[0], b_re[0], b_im[0], c_re[0], c_im[0], d_skip[0])
    n_state = SSM_GROUPS * SSM_STATE
    sg_p, ssr_p, ssi_p = s5_prompt(u_p, s5p)
    sg_s, ssr_s, ssi_s = s5_sample(u_s, state_ssm_re[0].reshape(S_ROWS, n_state),
                                   state_ssm_im[0].reshape(S_ROWS, n_state), s5p)
    yg_p, yg_s = ssm_glu(sg_p, sg_s, w_glu[0])

    merged_p, merged_s = merge(c_p, c_s, yg_p, yg_s, w_conv_out[0], w_ssm_out[0], gates_p, gates_s)
    x1_p, x1_s = oproj(merged_p, merged_s, w_o[0], xp, xs)

    xn2_p, xn2_s = rownorm_pair(x1_p, x1_s, row(norm_ffn_g[0]), BF16)
    ffn_old = [state_ffn_conv[0, :, k, :] for k in range(FFN_K - 1)]
    h_p, h_s, gate_tail, gate_s = ffn_up(xn2_p, xn2_s, w_up[0], ffn_old, ffn_conv_w[0], row(ffn_conv_b[0]))
    x2_p, x2_s = ffn_down(h_p, h_s, cast_bf16(w_down[0], 688), x1_p, x1_s)
    y_p, y_s = rownorm_pair(x2_p, x2_s, row(final_norm_g), F32)

    ffn_p = gate_tail[:, FFN_PAD - (FFN_K - 1):, :]
    ffn_s = jnp.stack([ffn_old[1], gate_s], axis=1)
    state_shape = (1, -1, SSM_GROUPS, SSM_STATE)
    return (y_p.reshape(BATCH, SEQ, D_MODEL), y_s.reshape(DEC_BATCH, 1, D_MODEL),
            conv_p[None], conv_s[None],
            ssr_p.reshape(state_shape), ssi_p.reshape(state_shape),
            ssr_s.reshape(state_shape), ssi_s.reshape(state_shape),
            ffn_p[None], ffn_s[None])
```

```python
import functools
import math

import jax
import jax.numpy as jnp
from jax import lax
from jax.experimental import pallas as pl
from jax.experimental.pallas import tpu as pltpu

D_MODEL = 4096
BATCH = 4
SEQ = 2048
DEC_BATCH = 128
CONV_WIDTH = D_MODEL // 2
CONV_K = 31
SSM_WIDTH = D_MODEL // 2
SSM_GROUP = 16
SSM_GROUPS = SSM_WIDTH // SSM_GROUP
SSM_STATE = 64
D_FF = 11008
FFN_K = 3
EPS = 1e-6

P_ROWS = BATCH * SEQ
S_ROWS = DEC_BATCH

V7X_SUBLANES = 8
V7X_LANES = 128
V7X_VMEM_LIMIT = 56 * 1024 * 1024

TM = SEQ
NPT = P_ROWS // TM
NSUB = 4
SUB = TM // NSUB

BF16 = jnp.bfloat16
F32 = jnp.float32


def _params(n_grid_dims):
    return pltpu.CompilerParams(
        dimension_semantics=("arbitrary",) * n_grid_dims,
        vmem_limit_bytes=V7X_VMEM_LIMIT)


def _dot(a, b):
    return jnp.dot(a, b, preferred_element_type=F32)


def _rms(x, g):
    return x * lax.rsqrt(jnp.mean(x * x, axis=-1, keepdims=True) + EPS) * g


def _interleave(n, mm, epi):
    mm(0)
    for k in range(1, n):
        mm(k)
        epi(k - 1)
    epi(n - 1)


def _sub_rows(k):
    return slice(k * SUB, (k + 1) * SUB)


class _RowTile:
    def __init__(self, hbm_refs, bufs, sems, ncol):
        self.hbm_refs, self.bufs, self.sems, self.ncol = hbm_refs, bufs, sems, ncol
        self.i, self.j = pl.program_id(0), pl.program_id(1)

    def _copies(self, tile, k):
        return [pltpu.make_async_copy(hbm.at[pl.ds(tile * TM + k * SUB, SUB), :],
                                      buf.at[pl.ds(k * SUB, SUB), :], sem.at[k])
                for hbm, buf, sem in zip(self.hbm_refs, self.bufs, self.sems)]

    def fetch_first_tile(self):
        @pl.when((self.i == 0) & (self.j == 0))
        def _():
            for k in range(NSUB):
                for c in self._copies(0, k):
                    c.start()

    def wait(self, k):
        for c in self._copies(self.i, k):
            c.wait()

    def fetch_next(self, k):
        @pl.when(self.i < NPT - 1)
        def _():
            for c in self._copies(self.i + 1, k):
                c.start()

    def run(self, body):
        assert self.ncol >= 2
        self.fetch_first_tile()
        pl.when(self.j == 0)(functools.partial(body, "first"))
        pl.when((self.j > 0) & (self.j < self.ncol - 1))(functools.partial(body, "mid"))
        pl.when(self.j == self.ncol - 1)(functools.partial(body, "last"))

    def mm_hooks(self, col, mm):
        def wrapped(k):
            if col == "first":
                self.wait(k)
            mm(k)
            if col == "last":
                self.fetch_next(k)
        return wrapped


_ANY = pl.BlockSpec(memory_space=pl.ANY)


def _row_tile_scratch(widths, dtype=BF16):
    return ([pltpu.VMEM((TM, w), dtype) for w in widths]
            + [pltpu.SemaphoreType.DMA((NSUB,)) for _ in widths])


def _resident(shape, index_map):
    return pl.BlockSpec(shape, index_map, pipeline_mode=pl.Buffered(1))


def _sample_cols(i, j):
    return (0, jnp.where(i == NPT - 1, j, 0))


def _on_last_row_tile(fn):
    pl.when(pl.program_id(0) == NPT - 1)(fn)


def _rownorm_kernel(x_ref, g_ref, o_ref):
    o_ref[...] = _rms(x_ref[...], g_ref[...]).astype(o_ref.dtype)


def rownorm(x, g, out_dtype, row_block):
    rows = x.shape[0]
    return pl.pallas_call(
        _rownorm_kernel,
        out_shape=jax.ShapeDtypeStruct((rows, D_MODEL), out_dtype),
        grid=(rows // row_block,),
        in_specs=[
            pl.BlockSpec((row_block, D_MODEL), lambda i: (i, 0)),
            pl.BlockSpec((1, D_MODEL), lambda i: (0, 0)),
        ],
        out_specs=pl.BlockSpec((row_block, D_MODEL), lambda i: (i, 0)),
        compiler_params=_params(1),
        name="rownorm",
    )(x, g)


def rownorm_pair(xp, xs, g, out_dtype):
    return rownorm(xp, g, out_dtype, 512), rownorm(xs, g, out_dtype, S_ROWS)


GLU_TN = 256


def _inproj_glu_kernel(ap_hbm, as_ref, wa_ref, wg_ref, op_ref, os_ref, wc_ref, r_ref, ap_ref, sem):
    tn = GLU_TN
    tile = _RowTile([ap_hbm], [ap_ref], [sem], CONV_WIDTH // tn)

    def body(col):
        wc_ref[:, 0:tn] = wa_ref[...].astype(BF16)
        wc_ref[:, tn:2 * tn] = wg_ref[...].astype(BF16)

        def mm(k):
            r_ref[_sub_rows(k), :] = _dot(ap_ref[_sub_rows(k), :], wc_ref[...])

        def epi(k):
            r = r_ref[_sub_rows(k), :]
            op_ref[_sub_rows(k), :] = r[:, 0:tn] * jax.nn.sigmoid(r[:, tn:2 * tn])

        _interleave(NSUB, tile.mm_hooks(col, mm), epi)

        @_on_last_row_tile
        def _():
            r = _dot(as_ref[...], wc_ref[...])
            os_ref[...] = r[:, 0:tn] * jax.nn.sigmoid(r[:, tn:2 * tn])

    tile.run(body)


def inproj_glu(xn_p, xn_s, w_in):
    tn = GLU_TN
    nj = CONV_WIDTH // tn
    return pl.pallas_call(
        _inproj_glu_kernel,
        out_shape=(jax.ShapeDtypeStruct((P_ROWS, CONV_WIDTH), F32),
                   jax.ShapeDtypeStruct((S_ROWS, CONV_WIDTH), F32)),
        grid=(NPT, nj),
        in_specs=[
            _ANY,
            _resident((S_ROWS, D_MODEL), lambda i, j: (0, 0)),
            pl.BlockSpec((D_MODEL, tn), lambda i, j: (0, j)),
            pl.BlockSpec((D_MODEL, tn), lambda i, j: (0, j + nj)),
        ],
        out_specs=(pl.BlockSpec((TM, tn), lambda i, j: (i, j)),
                   pl.BlockSpec((S_ROWS, tn), _sample_cols)),
        scratch_shapes=[pltpu.VMEM((D_MODEL, 2 * tn), BF16), pltpu.VMEM((TM, 2 * tn), F32)]
        + _row_tile_scratch([D_MODEL]),
        compiler_params=_params(2),
        name="inproj_glu",
    )(xn_p, xn_s, w_in, w_in)


def _inproj_cols_kernel(ap_hbm, as_ref, w_ref, op_ref, os_ref, wc_ref, ap_ref, sem, *, squash, ncol):
    tile = _RowTile([ap_hbm], [ap_ref], [sem], ncol)

    def body(col):
        wc_ref[...] = w_ref[...].astype(BF16)

        def mm(k):
            op_ref[_sub_rows(k), :] = _dot(ap_ref[_sub_rows(k), :], wc_ref[...])

        def epi(k):
            if squash:
                op_ref[_sub_rows(k), :] = jax.nn.sigmoid(op_ref[_sub_rows(k), :])

        _interleave(NSUB, tile.mm_hooks(col, mm), epi)

        @_on_last_row_tile
        def _():
            r = _dot(as_ref[...], wc_ref[...])
            os_ref[...] = jax.nn.sigmoid(r) if squash else r

    tile.run(body)


def inproj_cols(xn_p, xn_s, w_in, col0, width, squash, name):
    tn = 512
    return pl.pallas_call(
        functools.partial(_inproj_cols_kernel, squash=squash, ncol=width // tn),
        out_shape=(jax.ShapeDtypeStruct((P_ROWS, width), F32),
                   jax.ShapeDtypeStruct((S_ROWS, width), F32)),
        grid=(NPT, width // tn),
        in_specs=[
            _ANY,
            _resident((S_ROWS, D_MODEL), lambda i, j: (0, 0)),
            pl.BlockSpec((D_MODEL, tn), lambda i, j: (0, j + col0 // tn)),
        ],
        out_specs=(pl.BlockSpec((TM, tn), lambda i, j: (i, j)),
                   pl.BlockSpec((S_ROWS, tn), _sample_cols)),
        scratch_shapes=[pltpu.VMEM((D_MODEL, tn), BF16)] + _row_tile_scratch([D_MODEL]),
        compiler_params=_params(2),
        name=name,
    )(xn_p, xn_s, w_in)


CONV_TT = 256
CONV_HALO = 32
CONV_RC = 32
CONV_LC = 256
CONV_PIECE = 32


def _ln_swish(c, g, b):
    mu = jnp.mean(c, axis=-1, keepdims=True)
    d = c - mu
    var = jnp.mean(d * d, axis=-1, keepdims=True)
    r = d * lax.rsqrt(var + EPS) * g + b
    return r * jax.nn.sigmoid(r)


def _conv_prompt_kernel(x_ref, w_ref, cb_ref, g_ref, b_ref, o_ref, st_ref, xs_ref, cbuf_ref, w8_ref):
    t = pl.program_id(1)
    nt = pl.num_programs(1)
    sub = V7X_SUBLANES

    @pl.when(t == 0)
    def _():
        xs_ref[0, 0:CONV_HALO, :] = jnp.zeros((CONV_HALO, CONV_WIDTH), F32)
        xs_ref[0, CONV_HALO + CONV_TT:CONV_HALO + CONV_TT + sub, :] = jnp.zeros((sub, CONV_WIDTH), F32)
        for k in range(CONV_K):
            w8_ref[k] = jnp.broadcast_to(w_ref[k:k + 1, :], (sub, CONV_WIDTH))

    xs_ref[0, CONV_HALO:CONV_HALO + CONV_TT, :] = x_ref[...]

    def shift_piece(p, carry):
        r = pl.multiple_of(p * CONV_PIECE, CONV_PIECE)
        piece = xs_ref[0, pl.ds(r, CONV_PIECE + sub), :]
        for m in range(1, sub):
            rolled = pltpu.roll(piece, CONV_PIECE + sub - m, axis=0)
            xs_ref[m, pl.ds(r, CONV_PIECE), :] = rolled[0:CONV_PIECE]
        return carry

    lax.fori_loop(0, (CONV_HALO + CONV_TT) // CONV_PIECE, shift_piece, 0)

    off = CONV_HALO - (CONV_K - 1)

    lane_chunks = [slice(l0, l0 + CONV_LC) for l0 in range(0, CONV_WIDTH, CONV_LC)]

    def chunk(c, carry):
        r0 = pl.multiple_of(c * CONV_RC, CONV_RC)
        part = jnp.zeros((CONV_RC, CONV_LC), F32)
        for lanes in lane_chunks:
            accs = [jnp.zeros((sub, CONV_LC), F32) for _ in range(CONV_RC // sub)]
            for k in range(CONV_K):
                o = off + k
                wk = w8_ref[k, :, lanes]
                for rg in range(CONV_RC // sub):
                    win = xs_ref[o % sub, pl.ds(r0 + (o // sub + rg) * sub, sub), lanes]
                    accs[rg] = accs[rg] + win * wk
            acc = jnp.concatenate(accs, axis=0) + cb_ref[:, lanes]
            cbuf_ref[:, lanes] = acc
            part = part + acc
        mu = jnp.sum(part, axis=-1, keepdims=True) / CONV_WIDTH
        part = jnp.zeros((CONV_RC, CONV_LC), F32)
        for lanes in lane_chunks:
            d = cbuf_ref[:, lanes] - mu
            part = part + d * d
        rstd = lax.rsqrt(jnp.sum(part, axis=-1, keepdims=True) / CONV_WIDTH + EPS)
        for lanes in lane_chunks:
            r = (cbuf_ref[:, lanes] - mu) * rstd * g_ref[:, lanes] + b_ref[:, lanes]
            o_ref[pl.ds(r0, CONV_RC), lanes] = (r * jax.nn.sigmoid(r)).astype(o_ref.dtype)
        return carry

    lax.fori_loop(0, CONV_TT // CONV_RC, chunk, 0)

    @pl.when(t == nt - 1)
    def _():
        st_ref[0] = xs_ref[off, CONV_TT:CONV_TT + CONV_K - 1, :]

    xs_ref[0, 0:CONV_HALO, :] = xs_ref[0, CONV_TT:CONV_TT + CONV_HALO, :]


def conv_prompt(glu, conv_w, conv_b, ln_g, ln_b):
    nt = SEQ // CONV_TT
    vec = pl.BlockSpec((1, CONV_WIDTH), lambda b, t: (0, 0))
    return pl.pallas_call(
        _conv_prompt_kernel,
        out_shape=(jax.ShapeDtypeStruct((P_ROWS, CONV_WIDTH), BF16),
                   jax.ShapeDtypeStruct((BATCH, CONV_K - 1, CONV_WIDTH), F32)),
        grid=(BATCH, nt),
        in_specs=[
            pl.BlockSpec((CONV_TT, CONV_WIDTH), lambda b, t: (b * nt + t, 0)),
            pl.BlockSpec((CONV_K, CONV_WIDTH), lambda b, t: (0, 0)),
            vec, vec, vec,
        ],
        out_specs=(pl.BlockSpec((CONV_TT, CONV_WIDTH), lambda b, t: (b * nt + t, 0)),
                   pl.BlockSpec((1, CONV_K - 1, CONV_WIDTH), lambda b, t: (b, 0, 0))),
        scratch_shapes=[
            pltpu.VMEM((V7X_SUBLANES, CONV_HALO + CONV_TT + V7X_SUBLANES, CONV_WIDTH), F32),
            pltpu.VMEM((CONV_RC, CONV_WIDTH), F32),
            pltpu.VMEM((CONV_K, V7X_SUBLANES, CONV_WIDTH), F32),
        ],
        compiler_params=_params(2),
        name="conv_prompt",
    )(glu, conv_w, conv_b, ln_g, ln_b)


CONV_SB = 16


def _conv_sample_kernel(x_ref, st_ref, w_ref, cb_ref, g_ref, b_ref, o_ref, nst_ref, cbuf_ref):
    nb = CONV_K - 1
    w_old = w_ref[0:nb, :]
    w_new = w_ref[nb:CONV_K, :]
    for s in range(CONV_SB):
        st = st_ref[s]
        x = x_ref[s:s + 1, :]
        cbuf_ref[s:s + 1, :] = jnp.sum(st * w_old, axis=0, keepdims=True) + x * w_new
        nst_ref[s, 0:nb - 1, :] = st_ref[s, 1:nb, :]
        nst_ref[s, nb - 1:nb, :] = x
    o_ref[...] = _ln_swish(cbuf_ref[...] + cb_ref[...], g_ref[...], b_ref[...]).astype(o_ref.dtype)


def conv_sample(glu, state_conv, conv_w, conv_b, ln_g, ln_b):
    vec = pl.BlockSpec((1, CONV_WIDTH), lambda i: (0, 0))
    st_spec = pl.BlockSpec((CONV_SB, CONV_K - 1, CONV_WIDTH), lambda i: (i, 0, 0))
    return pl.pallas_call(
        _conv_sample_kernel,
        out_shape=(jax.ShapeDtypeStruct((S_ROWS, CONV_WIDTH), BF16),
                   jax.ShapeDtypeStruct((S_ROWS, CONV_K - 1, CONV_WIDTH), F32)),
        grid=(S_ROWS // CONV_SB,),
        in_specs=[
            pl.BlockSpec((CONV_SB, CONV_WIDTH), lambda i: (i, 0)),
            st_spec,
            pl.BlockSpec((CONV_K, CONV_WIDTH), lambda i: (0, 0)),
            vec, vec, vec,
        ],
        out_specs=(pl.BlockSpec((CONV_SB, CONV_WIDTH), lambda i: (i, 0)), st_spec),
        scratch_shapes=[pltpu.VMEM((CONV_SB, CONV_WIDTH), F32)],
        compiler_params=_params(1),
        name="conv_sample",
    )(glu, state_conv, conv_w, conv_b, ln_g, ln_b)


S5_GB = 16
S5_CH = S5_GB * SSM_GROUP
S5_ST = S5_GB * SSM_STATE
S5_NB = SSM_GROUPS // S5_GB
S5_SEG = V7X_SUBLANES
S5_SEGLEN = SEQ // S5_SEG
S5_TC = 64
S5_NC = S5_SEGLEN // S5_TC
S5_CR = S5_TC * S5_SEG
S5_LH = S5_CH // V7X_LANES


def _s5_discretize(lr_ref, li_ref, ldt_ref):
    lr, li = lr_ref[0], li_ref[0]
    dt = jnp.exp(ldt_ref[0])
    mag = jnp.exp(lr * dt)
    a_re, a_im = mag * jnp.cos(li * dt), mag * jnp.sin(li * dt)
    den = lr * lr + li * li
    nr, ni = a_re - 1.0, a_im
    f_re = (nr * lr + ni * li) / den
    f_im = (ni * lr - nr * li) / den
    return a_re, a_im, f_re, f_im


def _s5_fill_weights(f_re, f_im, bdre_ref, bdim_ref, cdre_ref, cdim_ref, bb_ref, cc_ref):
    bdre, bdim = bdre_ref[0], bdim_ref[0]
    bb_ref[:, 0:S5_ST] = (f_re * bdre - f_im * bdim).astype(BF16)
    bb_ref[:, S5_ST:2 * S5_ST] = (f_re * bdim + f_im * bdre).astype(BF16)
    cc_ref[0:S5_ST, :] = cdre_ref[0].astype(BF16)
    cc_ref[S5_ST:2 * S5_ST, :] = (-cdim_ref[0]).astype(BF16)


def _s5_prompt_kernel(u_ref, lr_ref, li_ref, ldt_ref, bdre_ref, bdim_ref, cdre_ref, cdim_ref, d_ref,
                      sg_ref, sgb_ref, hre_ref, him_ref,
                      bb_ref, cc_ref, ul_ref, sgl_ref, lhs_ref, bu_ref, hch_ref):
    a_re, a_im, f_re, f_im = _s5_discretize(lr_ref, li_ref, ldt_ref)
    _s5_fill_weights(f_re, f_im, bdre_ref, bdim_ref, cdre_ref, cdim_ref, bb_ref, cc_ref)
    are8 = jnp.broadcast_to(a_re, (S5_SEG, S5_ST))
    aim8 = jnp.broadcast_to(a_im, (S5_SEG, S5_ST))
    d = d_ref[0]
    for hh in range(S5_LH):
        ul_ref[hh] = u_ref[:, hh * V7X_LANES:(hh + 1) * V7X_LANES]

    def chunk_rows(c):
        return slice(c * S5_CR, (c + 1) * S5_CR)

    def project_in(c):
        for tl in range(S5_TC):
            t = c * S5_TC + tl
            for hh in range(S5_LH):
                lhs_ref[t * S5_SEG:(t + 1) * S5_SEG, hh * V7X_LANES:(hh + 1) * V7X_LANES] = (
                    ul_ref[hh, pl.ds(t, S5_SEG, stride=S5_SEGLEN), :])
        bu_ref[chunk_rows(c), :] = _dot(lhs_ref[chunk_rows(c), :].astype(BF16), bb_ref[...])

    def scan(c, hr, hi, keep):
        for tl in range(S5_TC):
            t = c * S5_TC + tl
            rows = slice(t * S5_SEG, (t + 1) * S5_SEG)
            br = bu_ref[rows, 0:S5_ST]
            bi = bu_ref[rows, S5_ST:2 * S5_ST]
            hr, hi = are8 * hr - aim8 * hi + br, are8 * hi + aim8 * hr + bi
            if keep:
                crow = slice(tl * S5_SEG, (tl + 1) * S5_SEG)
                hch_ref[c % 2, crow, 0:S5_ST] = hr
                hch_ref[c % 2, crow, S5_ST:2 * S5_ST] = hi
        return hr, hi

    def project_out(c):
        y = _dot(hch_ref[c % 2].astype(BF16), cc_ref[...]) + d * lhs_ref[chunk_rows(c), :]
        lhs_ref[chunk_rows(c), :] = jax.nn.gelu(y)
        for tl in range(S5_TC):
            t = c * S5_TC + tl
            for hh in range(S5_LH):
                sgl_ref[hh, pl.ds(t, S5_SEG, stride=S5_SEGLEN), :] = (
                    lhs_ref[t * S5_SEG:(t + 1) * S5_SEG, hh * V7X_LANES:(hh + 1) * V7X_LANES])

    zeros = jnp.zeros((S5_SEG, S5_ST), F32)
    e_re, e_im = zeros, zeros
    project_in(0)
    for c in range(S5_NC):
        if c + 1 < S5_NC:
            project_in(c + 1)
        e_re, e_im = scan(c, e_re, e_im, keep=False)

    p_re, p_im = a_re, a_im
    for _ in range(int(math.log2(S5_SEGLEN))):
        p_re, p_im = p_re * p_re - p_im * p_im, 2.0 * p_re * p_im
    seg = lax.broadcasted_iota(jnp.int32, (S5_SEG, S5_ST), 0)
    qr = qi = jnp.zeros((1, S5_ST), F32)
    h_re = h_im = zeros
    for s in range(1, S5_SEG):
        qr, qi = (p_re * qr - p_im * qi + e_re[s - 1:s, :],
                  p_re * qi + p_im * qr + e_im[s - 1:s, :])
        h_re = jnp.where(seg == s, qr, h_re)
        h_im = jnp.where(seg == s, qi, h_im)

    for c in range(S5_NC):
        h_re, h_im = scan(c, h_re, h_im, keep=True)
        if c > 0:
            project_out(c - 1)
    project_out(S5_NC - 1)

    hre_ref[0] = h_re[S5_SEG - 1:S5_SEG, :]
    him_ref[0] = h_im[S5_SEG - 1:S5_SEG, :]
    for hh in range(S5_LH):
        sg_ref[:, hh * V7X_LANES:(hh + 1) * V7X_LANES] = sgl_ref[hh]
        sgb_ref[:, hh * V7X_LANES:(hh + 1) * V7X_LANES] = sgl_ref[hh].astype(BF16)


def _s5_param_specs(idx):
    def spec(shape):
        return pl.BlockSpec((1,) + shape, lambda *g: (idx(*g), 0, 0))
    return [spec((1, S5_ST)), spec((1, S5_ST)), spec((1, S5_ST)),
            spec((S5_CH, S5_ST)), spec((S5_CH, S5_ST)),
            spec((S5_ST, S5_CH)), spec((S5_ST, S5_CH)), spec((1, S5_CH))]


def s5_prompt(u, s5p):
    return pl.pallas_call(
        _s5_prompt_kernel,
        out_shape=(jax.ShapeDtypeStruct((P_ROWS, SSM_WIDTH), F32),
                   jax.ShapeDtypeStruct((P_ROWS, SSM_WIDTH), BF16),
                   jax.ShapeDtypeStruct((BATCH, 1, SSM_GROUPS * SSM_STATE), F32),
                   jax.ShapeDtypeStruct((BATCH, 1, SSM_GROUPS * SSM_STATE), F32)),
        grid=(S5_NB, BATCH),
        in_specs=[pl.BlockSpec((SEQ, S5_CH), lambda j, b: (b, j))] + _s5_param_specs(lambda j, b: j),
        out_specs=(pl.BlockSpec((SEQ, S5_CH), lambda j, b: (b, j)),
                   pl.BlockSpec((SEQ, S5_CH), lambda j, b: (b, j)),
                   pl.BlockSpec((1, 1, S5_ST), lambda j, b: (b, 0, j)),
                   pl.BlockSpec((1, 1, S5_ST), lambda j, b: (b, 0, j))),
        scratch_shapes=[
            pltpu.VMEM((S5_CH, 2 * S5_ST), BF16),
            pltpu.VMEM((2 * S5_ST, S5_CH), BF16),
            pltpu.VMEM((S5_LH, SEQ, V7X_LANES), F32),
            pltpu.VMEM((S5_LH, SEQ, V7X_LANES), F32),
            pltpu.VMEM((SEQ, S5_CH), F32),
            pltpu.VMEM((SEQ, 2 * S5_ST), F32),
            pltpu.VMEM((2, S5_CR, 2 * S5_ST), F32),
        ],
        compiler_params=_params(2),
        name="s5_prompt",
    )(u, *s5p)


def _s5_sample_kernel(u_ref, h0re_ref, h0im_ref, lr_ref, li_ref, ldt_ref, bdre_ref, bdim_ref,
                      cdre_ref, cdim_ref, d_ref, sg_ref, sgb_ref, hre_ref, him_ref, bb_ref, cc_ref):
    a_re, a_im, f_re, f_im = _s5_discretize(lr_ref, li_ref, ldt_ref)
    _s5_fill_weights(f_re, f_im, bdre_ref, bdim_ref, cdre_ref, cdim_ref, bb_ref, cc_ref)
    u = u_ref[...]
    bu = _dot(u.astype(BF16), bb_ref[...])
    h0r, h0i = h0re_ref[...], h0im_ref[...]
    hr = a_re * h0r - a_im * h0i + bu[:, 0:S5_ST]
    hi = a_re * h0i + a_im * h0r + bu[:, S5_ST:2 * S5_ST]
    hre_ref[...] = hr
    him_ref[...] = hi
    y = _dot(hr.astype(BF16), cc_ref[0:S5_ST, :]) + _dot(hi.astype(BF16), cc_ref[S5_ST:2 * S5_ST, :])
    sg = jax.nn.gelu(y + d_ref[0] * u)
    sg_ref[...] = sg
    sgb_ref[...] = sg.astype(BF16)


def s5_sample(u, h0_re, h0_im, s5p):
    st = pl.BlockSpec((S_ROWS, S5_ST), lambda j: (0, j))
    return pl.pallas_call(
        _s5_sample_kernel,
        out_shape=(jax.ShapeDtypeStruct((S_ROWS, SSM_WIDTH), F32),
                   jax.ShapeDtypeStruct((S_ROWS, SSM_WIDTH), BF16),
                   jax.ShapeDtypeStruct((S_ROWS, SSM_GROUPS * SSM_STATE), F32),
                   jax.ShapeDtypeStruct((S_ROWS, SSM_GROUPS * SSM_STATE), F32)),
        grid=(S5_NB,),
        in_specs=[pl.BlockSpec((S_ROWS, S5_CH), lambda j: (0, j)), st, st]
        + _s5_param_specs(lambda j: j),
        out_specs=(pl.BlockSpec((S_ROWS, S5_CH), lambda j: (0, j)),
                   pl.BlockSpec((S_ROWS, S5_CH), lambda j: (0, j)), st, st),
        scratch_shapes=[
            pltpu.VMEM((S5_CH, 2 * S5_ST), BF16),
            pltpu.VMEM((2 * S5_ST, S5_CH), BF16),
        ],
        compiler_params=_params(1),
        name="s5_sample",
    )(u, h0_re, h0_im, *s5p)


def s5_block_params(lam_re, lam_im, log_dt, b_re, b_im, c_re, c_im, d_skip):
    eye = jnp.eye(S5_GB, dtype=bool)

    def bd(b):
        b4 = b.reshape(S5_NB, S5_GB, SSM_STATE, SSM_GROUP).transpose(0, 1, 3, 2)
        m = jnp.where(eye[None, :, None, :, None], b4[:, :, :, None, :], 0.0)
        return m.reshape(S5_NB, S5_CH, S5_ST)

    def cd(c):
        c4 = c.reshape(S5_NB, S5_GB, SSM_GROUP, SSM_STATE).transpose(0, 1, 3, 2)
        m = jnp.where(eye[None, :, None, :, None], c4[:, :, :, None, :], 0.0)
        return m.reshape(S5_NB, S5_ST, S5_CH)

    return (lam_re.reshape(S5_NB, 1, S5_ST), lam_im.reshape(S5_NB, 1, S5_ST),
            jnp.repeat(log_dt, SSM_STATE).reshape(S5_NB, 1, S5_ST),
            bd(b_re), bd(b_im), cd(c_re), cd(c_im), d_skip.reshape(S5_NB, 1, S5_CH))


SSM_GLU_TN = 512


def _ssm_glu_kernel(ap_hbm, as_ref, w_ref, tp_ref, ts_ref, op_ref, os_ref, wc_ref, r_ref, ap_ref, sem):
    tile = _RowTile([ap_hbm], [ap_ref], [sem], SSM_WIDTH // SSM_GLU_TN)

    def body(col):
        wc_ref[...] = w_ref[...].astype(BF16)

        def mm(k):
            r_ref[_sub_rows(k), :] = _dot(ap_ref[_sub_rows(k), :], wc_ref[...])

        def epi(k):
            op_ref[_sub_rows(k), :] = (
                tp_ref[_sub_rows(k), :] * jax.nn.sigmoid(r_ref[_sub_rows(k), :])).astype(op_ref.dtype)

        _interleave(NSUB, tile.mm_hooks(col, mm), epi)

        @_on_last_row_tile
        def _():
            z = _dot(as_ref[...], wc_ref[...])
            os_ref[...] = (ts_ref[...] * jax.nn.sigmoid(z)).astype(os_ref.dtype)

    tile.run(body)


def ssm_glu(sg_p, sg_s, sgb_p, sgb_s, w_glu):
    tn = SSM_GLU_TN
    return pl.pallas_call(
        _ssm_glu_kernel,
        out_shape=(jax.ShapeDtypeStruct((P_ROWS, SSM_WIDTH), BF16),
                   jax.ShapeDtypeStruct((S_ROWS, SSM_WIDTH), BF16)),
        grid=(NPT, SSM_WIDTH // tn),
        in_specs=[
            _ANY,
            _resident((S_ROWS, SSM_WIDTH), lambda i, j: (0, 0)),
            pl.BlockSpec((SSM_WIDTH, tn), lambda i, j: (0, j)),
            pl.BlockSpec((TM, tn), lambda i, j: (i, j)),
            pl.BlockSpec((S_ROWS, tn), lambda i, j: (0, j)),
        ],
        out_specs=(pl.BlockSpec((TM, tn), lambda i, j: (i, j)),
                   pl.BlockSpec((S_ROWS, tn), _sample_cols)),
        scratch_shapes=[pltpu.VMEM((SSM_WIDTH, tn), BF16), pltpu.VMEM((TM, tn), F32)]
        + _row_tile_scratch([SSM_WIDTH]),
        compiler_params=_params(2),
        name="ssm_glu",
    )(sgb_p, sgb_s, w_glu, sg_p, sg_s)


MERGE_TN = 256


def _merge_kernel(cp_hbm, cs_ref, yp_hbm, ys_ref, wa_ref, wb_ref, gap_ref, gbp_ref, gas_ref, gbs_ref,
                  op_ref, os_ref, wc_ref, r_ref, cp_ref, yp_ref, csem, ysem):
    tile = _RowTile([cp_hbm, yp_hbm], [cp_ref, yp_ref], [csem, ysem], D_MODEL // MERGE_TN)

    def body(col):
        wc_ref[0] = wa_ref[...].astype(BF16)
        wc_ref[1] = wb_ref[...].astype(BF16)

        def mm(k):
            r_ref[0, _sub_rows(k), :] = _dot(cp_ref[_sub_rows(k), :], wc_ref[0])
            r_ref[1, _sub_rows(k), :] = _dot(yp_ref[_sub_rows(k), :], wc_ref[1])

        def epi(k):
            rows = _sub_rows(k)
            op_ref[rows, :] = (gap_ref[rows, :] * r_ref[0, rows, :]
                               + gbp_ref[rows, :] * r_ref[1, rows, :]).astype(op_ref.dtype)

        _interleave(NSUB, tile.mm_hooks(col, mm), epi)

        @_on_last_row_tile
        def _():
            ya = _dot(cs_ref[...], wc_ref[0])
            yb = _dot(ys_ref[...], wc_ref[1])
            os_ref[...] = (gas_ref[...] * ya + gbs_ref[...] * yb).astype(os_ref.dtype)

    tile.run(body)


def merge(c_p, c_s, yg_p, yg_s, w_conv_out, w_ssm_out, gates_p, gates_s):
    tn = MERGE_TN
    nj = D_MODEL // tn
    return pl.pallas_call(
        _merge_kernel,
        out_shape=(jax.ShapeDtypeStruct((P_ROWS, D_MODEL), BF16),
                   jax.ShapeDtypeStruct((S_ROWS, D_MODEL), BF16)),
        grid=(NPT, nj),
        in_specs=[
            _ANY,
            _resident((S_ROWS, CONV_WIDTH), lambda i, j: (0, 0)),
            _ANY,
            _resident((S_ROWS, SSM_WIDTH), lambda i, j: (0, 0)),
            pl.BlockSpec((CONV_WIDTH, tn), lambda i, j: (0, j)),
            pl.BlockSpec((SSM_WIDTH, tn), lambda i, j: (0, j)),
            pl.BlockSpec((TM, tn), lambda i, j: (i, j)),
            pl.BlockSpec((TM, tn), lambda i, j: (i, j + nj)),
            pl.BlockSpec((S_ROWS, tn), lambda i, j: (0, j)),
            pl.BlockSpec((S_ROWS, tn), lambda i, j: (0, j + nj)),
        ],
        out_specs=(pl.BlockSpec((TM, tn), lambda i, j: (i, j)),
                   pl.BlockSpec((S_ROWS, tn), _sample_cols)),
        scratch_shapes=[pltpu.VMEM((2, CONV_WIDTH, tn), BF16), pltpu.VMEM((2, TM, tn), F32)]
        + _row_tile_scratch([CONV_WIDTH, SSM_WIDTH]),
        compiler_params=_params(2),
        name="merge",
    )(c_p, c_s, yg_p, yg_s, w_conv_out, w_ssm_out, gates_p, gates_p, gates_s, gates_s)


OPROJ_TN = 512


def _oproj_kernel(ap_hbm, as_ref, w_ref, xp_ref, xs_ref, op_ref, os_ref, wc_ref, ap_ref, sem):
    tile = _RowTile([ap_hbm], [ap_ref], [sem], D_MODEL // OPROJ_TN)

    def body(col):
        wc_ref[...] = w_ref[...].astype(BF16)

        def mm(k):
            op_ref[_sub_rows(k), :] = _dot(ap_ref[_sub_rows(k), :], wc_ref[...])

        def epi(k):
            op_ref[_sub_rows(k), :] = op_ref[_sub_rows(k), :] + xp_ref[_sub_rows(k), :]

        _interleave(NSUB, tile.mm_hooks(col, mm), epi)

        @_on_last_row_tile
        def _():
            os_ref[...] = xs_ref[...] + _dot(as_ref[...], wc_ref[...])

    tile.run(body)


def oproj(merged_p, merged_s, w_o, xp, xs):
    tn = OPROJ_TN
    return pl.pallas_call(
        _oproj_kernel,
        out_shape=(jax.ShapeDtypeStruct((P_ROWS, D_MODEL), F32),
                   jax.ShapeDtypeStruct((S_ROWS, D_MODEL), F32)),
        grid=(NPT, D_MODEL // tn),
        in_specs=[
            _ANY,
            _resident((S_ROWS, D_MODEL), lambda i, j: (0, 0)),
            pl.BlockSpec((D_MODEL, tn), lambda i, j: (0, j)),
            pl.BlockSpec((TM, tn), lambda i, j: (i, j)),
            pl.BlockSpec((S_ROWS, tn), lambda i, j: (0, j)),
        ],
        out_specs=(pl.BlockSpec((TM, tn), lambda i, j: (i, j)),
                   pl.BlockSpec((S_ROWS, tn), _sample_cols)),
        scratch_shapes=[pltpu.VMEM((D_MODEL, tn), BF16)] + _row_tile_scratch([D_MODEL]),
        compiler_params=_params(2),
        name="oproj",
    )(merged_p, merged_s, w_o, xp, xs)


def _cast_kernel(x_ref, o_ref):
    o_ref[...] = x_ref[...].astype(o_ref.dtype)


def cast_bf16(w, row_block):
    r, c = w.shape
    return pl.pallas_call(
        _cast_kernel,
        out_shape=jax.ShapeDtypeStruct((r, c), BF16),
        grid=(r // row_block,),
        in_specs=[pl.BlockSpec((row_block, c), lambda i: (i, 0))],
        out_specs=pl.BlockSpec((row_block, c), lambda i: (i, 0)),
        compiler_params=_params(1),
        name="cast_bf16",
    )(w)


DOWN_TM = 512
DOWN_NPT = P_ROWS // DOWN_TM


def _ffn_down_kernel(ap_ref, as_ref, w_ref, xp_ref, xs_ref, op_ref, os_ref):
    op_ref[...] = xp_ref[...] + _dot(ap_ref[...], w_ref[...])

    @pl.when(pl.program_id(0) == DOWN_NPT - 1)
    def _():
        os_ref[...] = xs_ref[...] + _dot(as_ref[...], w_ref[...])


def ffn_down(h_p, h_s, w_down_bf16, x1_p, x1_s):
    tn = 512
    last = DOWN_NPT - 1
    return pl.pallas_call(
        _ffn_down_kernel,
        out_shape=(jax.ShapeDtypeStruct((P_ROWS, D_MODEL), F32),
                   jax.ShapeDtypeStruct((S_ROWS, D_MODEL), F32)),
        grid=(DOWN_NPT, D_MODEL // tn),
        in_specs=[
            pl.BlockSpec((DOWN_TM, D_FF), lambda i, j: (i, 0)),
            _resident((S_ROWS, D_FF), lambda i, j: (0, 0)),
            pl.BlockSpec((D_FF, tn), lambda i, j: (0, j)),
            pl.BlockSpec((DOWN_TM, tn), lambda i, j: (i, j)),
            pl.BlockSpec((S_ROWS, tn), lambda i, j: (0, j)),
        ],
        out_specs=(pl.BlockSpec((DOWN_TM, tn), lambda i, j: (i, j)),
                   pl.BlockSpec((S_ROWS, tn), lambda i, j: (0, jnp.where(i == last, j, 0)))),
        compiler_params=_params(2),
        name="ffn_down",
    )(h_p, h_s, w_down_bf16, x1_p, x1_s)


FFN_TN = 256
FFN_NJ = D_FF // FFN_TN
FFN_PAD = V7X_SUBLANES


def _ffn_up_kernel(ap_hbm, as_ref, wg_ref, wv_ref, s0_ref, s1_ref, cw_ref, cb_ref,
                   hp_ref, hs_ref, tail_ref, gs_ref, wc_ref, g_ref, v_ref, ap_ref, sem):
    tn = FFN_TN
    tile = _RowTile([ap_hbm], [ap_ref], [sem], FFN_NJ)

    def body(col):
        wc_ref[:, 0:tn] = wg_ref[...].astype(BF16)
        wc_ref[:, tn:2 * tn] = wv_ref[...].astype(BF16)
        w0, w1, w2 = cw_ref[0:1, :], cw_ref[1:2, :], cw_ref[2:3, :]
        cb = cb_ref[...]
        g_ref[0:FFN_PAD, :] = jnp.zeros((FFN_PAD, tn), F32)

        def mm(k):
            r = _dot(ap_ref[_sub_rows(k), :], wc_ref[...])
            g_ref[FFN_PAD + k * SUB:FFN_PAD + (k + 1) * SUB, :] = r[:, 0:tn]
            v_ref[_sub_rows(k), :] = r[:, tn:2 * tn]

        def epi(k):
            r0 = FFN_PAD + k * SUB
            gc = (w0 * g_ref[r0 - 2:r0 - 2 + SUB, :] + w1 * g_ref[r0 - 1:r0 - 1 + SUB, :]
                  + w2 * g_ref[r0:r0 + SUB, :] + cb)
            hp_ref[_sub_rows(k), :] = (gc * jax.nn.sigmoid(gc) * v_ref[_sub_rows(k), :]).astype(hp_ref.dtype)

        _interleave(NSUB, tile.mm_hooks(col, mm), epi)
        tail_ref[0] = g_ref[TM:TM + FFN_PAD, :]

        @_on_last_row_tile
        def _():
            r = _dot(as_ref[...], wc_ref[...])
            gate, val = r[:, 0:tn], r[:, tn:2 * tn]
            gc = w0 * s0_ref[...] + w1 * s1_ref[...] + w2 * gate + cb
            hs_ref[...] = (gc * jax.nn.sigmoid(gc) * val).astype(hs_ref.dtype)
            gs_ref[...] = gate

    tile.run(body)


def ffn_up(xn2_p, xn2_s, w_up, ffn_old, ffn_conv_w, ffn_conv_b):
    nj = FFN_NJ
    tn = FFN_TN
    return pl.pallas_call(
        _ffn_up_kernel,
        out_shape=(jax.ShapeDtypeStruct((P_ROWS, D_FF), BF16),
                   jax.ShapeDtypeStruct((S_ROWS, D_FF), BF16),
                   jax.ShapeDtypeStruct((NPT, FFN_PAD, D_FF), F32),
                   jax.ShapeDtypeStruct((S_ROWS, D_FF), F32)),
        grid=(NPT, nj),
        in_specs=[
            _ANY,
            _resident((S_ROWS, D_MODEL), lambda i, j: (0, 0)),
            pl.BlockSpec((D_MODEL, tn), lambda i, j: (0, j)),
            pl.BlockSpec((D_MODEL, tn), lambda i, j: (0, j + nj)),
            pl.BlockSpec((S_ROWS, tn), lambda i, j: (0, j)),
            pl.BlockSpec((S_ROWS, tn), lambda i, j: (0, j)),
            pl.BlockSpec((FFN_K, tn), lambda i, j: (0, j)),
            pl.BlockSpec((1, tn), lambda i, j: (0, j)),
        ],
        out_specs=(
            pl.BlockSpec((TM, tn), lambda i, j: (i, j)),
            pl.BlockSpec((S_ROWS, tn), _sample_cols),
            pl.BlockSpec((1, FFN_PAD, tn), lambda i, j: (i, 0, j)),
            pl.BlockSpec((S_ROWS, tn), _sample_cols),
        ),
        scratch_shapes=[
            pltpu.VMEM((D_MODEL, 2 * tn), BF16),
            pltpu.VMEM((FFN_PAD + TM, tn), F32),
            pltpu.VMEM((TM, tn), F32),
        ] + _row_tile_scratch([D_MODEL]),
        compiler_params=_params(2),
        name="ffn_up",
    )(xn2_p, xn2_s, w_up, w_up, ffn_old[0], ffn_old[1], ffn_conv_w, ffn_conv_b)


def kernel(x_prompt, x_sample, state_conv, state_ssm_re, state_ssm_im, state_ffn_conv,
           norm_mix_g, w_in, conv_w, conv_b, ln_g, ln_b, w_conv_out,
           lam_re, lam_im, log_dt, b_re, b_im, c_re, c_im, d_skip, w_glu, w_ssm_out, w_o,
           norm_ffn_g, w_up, ffn_conv_w, ffn_conv_b, w_down, final_norm_g):
    xp = x_prompt.reshape(P_ROWS, D_MODEL)
    xs = x_sample.reshape(S_ROWS, D_MODEL)

    def row(v):
        return v.reshape(1, -1)

    xn_p, xn_s = rownorm_pair(xp, xs, row(norm_mix_g[0]), BF16)
    glu_p, glu_s = inproj_glu(xn_p, xn_s, w_in[0])
    u_p, u_s = inproj_cols(xn_p, xn_s, w_in[0], 2 * CONV_WIDTH, SSM_WIDTH, False, "inproj_ssm")
    gates_p, gates_s = inproj_cols(xn_p, xn_s, w_in[0], 2 * CONV_WIDTH + SSM_WIDTH, 2 * D_MODEL, True,
                                   "inproj_gates")

    conv_vecs = (conv_w[0], row(conv_b[0]), row(ln_g[0]), row(ln_b[0]))
    c_p, conv_p = conv_prompt(glu_p, *conv_vecs)
    c_s, conv_s = conv_sample(glu_s, state_conv[0], *conv_vecs)

    s5p = s5_block_params(lam_re[0], lam_im[0], log_dt[0], b_re[0], b_im[0], c_re[0], c_im[0], d_skip[0])
    n_state = SSM_GROUPS * SSM_STATE
    sg_p, sgb_p, ssr_p, ssi_p = s5_prompt(u_p, s5p)
    sg_s, sgb_s, ssr_s, ssi_s = s5_sample(u_s, state_ssm_re[0].reshape(S_ROWS, n_state),
                                          state_ssm_im[0].reshape(S_ROWS, n_state), s5p)
    yg_p, yg_s = ssm_glu(sg_p, sg_s, sgb_p, sgb_s, w_glu[0])

    merged_p, merged_s = merge(c_p, c_s, yg_p, yg_s, w_conv_out[0], w_ssm_out[0], gates_p, gates_s)
    x1_p, x1_s = oproj(merged_p, merged_s, w_o[0], xp, xs)

    xn2_p, xn2_s = rownorm_pair(x1_p, x1_s, row(norm_ffn_g[0]), BF16)
    ffn_old = [state_ffn_conv[0, :, k, :] for k in range(FFN_K - 1)]
    h_p, h_s, gate_tail, gate_s = ffn_up(xn2_p, xn2_s, w_up[0], ffn_old, ffn_conv_w[0], row(ffn_conv_b[0]))
    x2_p, x2_s = ffn_down(h_p, h_s, cast_bf16(w_down[0], 688), x1_p, x1_s)
    y_p, y_s = rownorm_pair(x2_p, x2_s, row(final_norm_g), F32)

    ffn_p = gate_tail[:, FFN_PAD - (FFN_K - 1):, :]
    ffn_s = jnp.stack([ffn_old[1], gate_s], axis=1)
    state_shape = (1, -1, SSM_GROUPS, SSM_STATE)
    return (y_p.reshape(BATCH, SEQ, D_MODEL), y_s.reshape(DEC_BATCH, 1, D_MODEL),
            conv_p[None], conv_s[None],
            ssr_p.reshape(state_shape), ssi_p.reshape(state_shape),
            ssr_s.reshape(state_shape), ssi_s.reshape(state_shape),
            ffn_p[None], ffn_s[None])
```

```python
import functools
import math

import jax
import jax.numpy as jnp
from jax import lax
from jax.experimental import pallas as pl
from jax.experimental.pallas import tpu as pltpu

D_MODEL = 4096
BATCH = 4
SEQ = 2048
DEC_BATCH = 128
CONV_WIDTH = D_MODEL // 2
CONV_K = 31
SSM_WIDTH = D_MODEL // 2
SSM_GROUP = 16
SSM_GROUPS = SSM_WIDTH // SSM_GROUP
SSM_STATE = 64
D_FF = 11008
FFN_K = 3
EPS = 1e-6

P_ROWS = BATCH * SEQ
S_ROWS = DEC_BATCH

V7X_SUBLANES = 8
V7X_LANES = 128
V7X_VMEM_LIMIT = 58 * 1024 * 1024

TM = SEQ
NPT = P_ROWS // TM
NSUB = 4
SUB = TM // NSUB

BF16 = jnp.bfloat16
F32 = jnp.float32


def _params(n_grid_dims):
    return pltpu.CompilerParams(
        dimension_semantics=("arbitrary",) * n_grid_dims,
        vmem_limit_bytes=V7X_VMEM_LIMIT)


def _dot(a, b):
    return jnp.dot(a, b, preferred_element_type=F32)


def _rms(x, g):
    return x * lax.rsqrt(jnp.mean(x * x, axis=-1, keepdims=True) + EPS) * g


def _interleave(n, mm, epi):
    mm(0)
    for k in range(1, n):
        mm(k)
        epi(k - 1)
    epi(n - 1)


def _sub_rows(k):
    return slice(k * SUB, (k + 1) * SUB)


class _RowTile:
    def __init__(self, hbm_refs, bufs, sems, ncol, tm=TM, nsub=NSUB):
        self.hbm_refs, self.bufs, self.sems, self.ncol = hbm_refs, bufs, sems, ncol
        self.tm, self.nsub, self.sub = tm, nsub, tm // nsub
        self.ntiles = P_ROWS // tm
        self.i, self.j = pl.program_id(0), pl.program_id(1)

    def rows(self, k):
        return slice(k * self.sub, (k + 1) * self.sub)

    def _copies(self, tile, k):
        return [pltpu.make_async_copy(hbm.at[pl.ds(tile * self.tm + k * self.sub, self.sub), :],
                                      buf.at[pl.ds(k * self.sub, self.sub), :], sem.at[k])
                for hbm, buf, sem in zip(self.hbm_refs, self.bufs, self.sems)]

    def fetch_first_tile(self):
        @pl.when((self.i == 0) & (self.j == 0))
        def _():
            for k in range(self.nsub):
                for c in self._copies(0, k):
                    c.start()

    def wait(self, k):
        for c in self._copies(self.i, k):
            c.wait()

    def fetch_next(self, k):
        @pl.when(self.i < self.ntiles - 1)
        def _():
            for c in self._copies(self.i + 1, k):
                c.start()

    def run(self, body):
        assert self.ncol >= 2
        self.fetch_first_tile()
        pl.when(self.j == 0)(functools.partial(body, "first"))
        pl.when((self.j > 0) & (self.j < self.ncol - 1))(functools.partial(body, "mid"))
        pl.when(self.j == self.ncol - 1)(functools.partial(body, "last"))

    def mm_hooks(self, col, mm):
        def wrapped(k):
            if col == "first":
                self.wait(k)
            mm(k)
            if col == "last":
                self.fetch_next(k)
        return wrapped


_ANY = pl.BlockSpec(memory_space=pl.ANY)


def _row_tile_scratch(widths, tm=TM, nsub=NSUB):
    return ([pltpu.VMEM((tm, w), BF16) for w in widths]
            + [pltpu.SemaphoreType.DMA((nsub,)) for _ in widths])


def _resident(shape, index_map):
    return pl.BlockSpec(shape, index_map, pipeline_mode=pl.Buffered(1))


def _sample_cols(i, j):
    return (0, jnp.where(i == NPT - 1, j, 0))


def _on_last_row_tile(fn):
    pl.when(pl.program_id(0) == NPT - 1)(fn)


def _rownorm_kernel(x_ref, g_ref, o_ref):
    o_ref[...] = _rms(x_ref[...], g_ref[...]).astype(o_ref.dtype)


def rownorm(x, g, out_dtype, row_block):
    rows = x.shape[0]
    return pl.pallas_call(
        _rownorm_kernel,
        out_shape=jax.ShapeDtypeStruct((rows, D_MODEL), out_dtype),
        grid=(rows // row_block,),
        in_specs=[
            pl.BlockSpec((row_block, D_MODEL), lambda i: (i, 0)),
            pl.BlockSpec((1, D_MODEL), lambda i: (0, 0)),
        ],
        out_specs=pl.BlockSpec((row_block, D_MODEL), lambda i: (i, 0)),
        compiler_params=_params(1),
        name="rownorm",
    )(x, g)


def rownorm_pair(xp, xs, g, out_dtype):
    return rownorm(xp, g, out_dtype, 512), rownorm(xs, g, out_dtype, S_ROWS)


GLU_TN = 256


def _inproj_glu_kernel(ap_hbm, as_ref, wa_ref, wg_ref, op_ref, os_ref, wc_ref, r_ref, ap_ref, sem):
    tn = GLU_TN
    tile = _RowTile([ap_hbm], [ap_ref], [sem], CONV_WIDTH // tn)

    def body(col):
        wc_ref[:, 0:tn] = wa_ref[...].astype(BF16)
        wc_ref[:, tn:2 * tn] = wg_ref[...].astype(BF16)

        def mm(k):
            r_ref[_sub_rows(k), :] = _dot(ap_ref[_sub_rows(k), :], wc_ref[...])

        def epi(k):
            r = r_ref[_sub_rows(k), :]
            op_ref[_sub_rows(k), :] = r[:, 0:tn] * jax.nn.sigmoid(r[:, tn:2 * tn])

        _interleave(NSUB, tile.mm_hooks(col, mm), epi)

        @_on_last_row_tile
        def _():
            r = _dot(as_ref[...], wc_ref[...])
            os_ref[...] = r[:, 0:tn] * jax.nn.sigmoid(r[:, tn:2 * tn])

    tile.run(body)


def inproj_glu(xn_p, xn_s, w_in):
    tn = GLU_TN
    nj = CONV_WIDTH // tn
    return pl.pallas_call(
        _inproj_glu_kernel,
        out_shape=(jax.ShapeDtypeStruct((P_ROWS, CONV_WIDTH), F32),
                   jax.ShapeDtypeStruct((S_ROWS, CONV_WIDTH), F32)),
        grid=(NPT, nj),
        in_specs=[
            _ANY,
            _resident((S_ROWS, D_MODEL), lambda i, j: (0, 0)),
            pl.BlockSpec((D_MODEL, tn), lambda i, j: (0, j)),
            pl.BlockSpec((D_MODEL, tn), lambda i, j: (0, j + nj)),
        ],
        out_specs=(pl.BlockSpec((TM, tn), lambda i, j: (i, j)),
                   pl.BlockSpec((S_ROWS, tn), _sample_cols)),
        scratch_shapes=[pltpu.VMEM((D_MODEL, 2 * tn), BF16), pltpu.VMEM((TM, 2 * tn), F32)]
        + _row_tile_scratch([D_MODEL]),
        compiler_params=_params(2),
        name="inproj_glu",
    )(xn_p, xn_s, w_in, w_in)


def _inproj_cols_kernel(ap_hbm, as_ref, w_ref, op_ref, os_ref, wc_ref, ap_ref, sem, *, squash, ncol):
    tile = _RowTile([ap_hbm], [ap_ref], [sem], ncol)

    def body(col):
        wc_ref[...] = w_ref[...].astype(BF16)

        def mm(k):
            op_ref[_sub_rows(k), :] = _dot(ap_ref[_sub_rows(k), :], wc_ref[...])

        def epi(k):
            if squash:
                op_ref[_sub_rows(k), :] = jax.nn.sigmoid(op_ref[_sub_rows(k), :])

        _interleave(NSUB, tile.mm_hooks(col, mm), epi)

        @_on_last_row_tile
        def _():
            r = _dot(as_ref[...], wc_ref[...])
            os_ref[...] = jax.nn.sigmoid(r) if squash else r

    tile.run(body)


def inproj_cols(xn_p, xn_s, w_in, col0, width, squash, name):
    tn = 512
    return pl.pallas_call(
        functools.partial(_inproj_cols_kernel, squash=squash, ncol=width // tn),
        out_shape=(jax.ShapeDtypeStruct((P_ROWS, width), F32),
                   jax.ShapeDtypeStruct((S_ROWS, width), F32)),
        grid=(NPT, width // tn),
        in_specs=[
            _ANY,
            _resident((S_ROWS, D_MODEL), lambda i, j: (0, 0)),
            pl.BlockSpec((D_MODEL, tn), lambda i, j: (0, j + col0 // tn)),
        ],
        out_specs=(pl.BlockSpec((TM, tn), lambda i, j: (i, j)),
                   pl.BlockSpec((S_ROWS, tn), _sample_cols)),
        scratch_shapes=[pltpu.VMEM((D_MODEL, tn), BF16)] + _row_tile_scratch([D_MODEL]),
        compiler_params=_params(2),
        name=name,
    )(xn_p, xn_s, w_in)


CONV_TT = 256
CONV_HALO = 32
CONV_RC = 32
CONV_LC = 256
CONV_PIECE = 32


def _ln_swish(c, g, b):
    mu = jnp.mean(c, axis=-1, keepdims=True)
    d = c - mu
    var = jnp.mean(d * d, axis=-1, keepdims=True)
    r = d * lax.rsqrt(var + EPS) * g + b
    return r * jax.nn.sigmoid(r)


def _conv_prompt_kernel(x_ref, w_ref, cb_ref, g_ref, b_ref, o_ref, st_ref, xs_ref, cbuf_ref, w8_ref):
    t = pl.program_id(1)
    nt = pl.num_programs(1)
    sub = V7X_SUBLANES

    @pl.when(t == 0)
    def _():
        xs_ref[0, 0:CONV_HALO, :] = jnp.zeros((CONV_HALO, CONV_WIDTH), F32)
        xs_ref[0, CONV_HALO + CONV_TT:CONV_HALO + CONV_TT + sub, :] = jnp.zeros((sub, CONV_WIDTH), F32)
        for k in range(CONV_K):
            w8_ref[k] = jnp.broadcast_to(w_ref[k:k + 1, :], (sub, CONV_WIDTH))

    xs_ref[0, CONV_HALO:CONV_HALO + CONV_TT, :] = x_ref[...]

    def shift_piece(p, carry):
        r = pl.multiple_of(p * CONV_PIECE, CONV_PIECE)
        piece = xs_ref[0, pl.ds(r, CONV_PIECE + sub), :]
        for m in range(1, sub):
            rolled = pltpu.roll(piece, CONV_PIECE + sub - m, axis=0)
            xs_ref[m, pl.ds(r, CONV_PIECE), :] = rolled[0:CONV_PIECE]
        return carry

    lax.fori_loop(0, (CONV_HALO + CONV_TT) // CONV_PIECE, shift_piece, 0)

    off = CONV_HALO - (CONV_K - 1)

    lane_chunks = [slice(l0, l0 + CONV_LC) for l0 in range(0, CONV_WIDTH, CONV_LC)]

    def chunk(c, carry):
        r0 = pl.multiple_of(c * CONV_RC, CONV_RC)
        part = jnp.zeros((CONV_RC, CONV_LC), F32)
        for lanes in lane_chunks:
            accs = [jnp.zeros((sub, CONV_LC), F32) for _ in range(CONV_RC // sub)]
            for k in range(CONV_K):
                o = off + k
                wk = w8_ref[k, :, lanes]
                for rg in range(CONV_RC // sub):
                    win = xs_ref[o % sub, pl.ds(r0 + (o // sub + rg) * sub, sub), lanes]
                    accs[rg] = accs[rg] + win * wk
            acc = jnp.concatenate(accs, axis=0) + cb_ref[:, lanes]
            cbuf_ref[:, lanes] = acc
            part = part + acc
        mu = jnp.sum(part, axis=-1, keepdims=True) / CONV_WIDTH
        part = jnp.zeros((CONV_RC, CONV_LC), F32)
        for lanes in lane_chunks:
            d = cbuf_ref[:, lanes] - mu
            part = part + d * d
        rstd = lax.rsqrt(jnp.sum(part, axis=-1, keepdims=True) / CONV_WIDTH + EPS)
        for lanes in lane_chunks:
            r = (cbuf_ref[:, lanes] - mu) * rstd * g_ref[:, lanes] + b_ref[:, lanes]
            o_ref[pl.ds(r0, CONV_RC), lanes] = (r * jax.nn.sigmoid(r)).astype(o_ref.dtype)
        return carry

    lax.fori_loop(0, CONV_TT // CONV_RC, chunk, 0)

    @pl.when(t == nt - 1)
    def _():
        st_ref[0] = xs_ref[off, CONV_TT:CONV_TT + CONV_K - 1, :]

    xs_ref[0, 0:CONV_HALO, :] = xs_ref[0, CONV_TT:CONV_TT + CONV_HALO, :]


def conv_prompt(glu, conv_w, conv_b, ln_g, ln_b):
    nt = SEQ // CONV_TT
    vec = pl.BlockSpec((1, CONV_WIDTH), lambda b, t: (0, 0))
    return pl.pallas_call(
        _conv_prompt_kernel,
        out_shape=(jax.ShapeDtypeStruct((P_ROWS, CONV_WIDTH), BF16),
                   jax.ShapeDtypeStruct((BATCH, CONV_K - 1, CONV_WIDTH), F32)),
        grid=(BATCH, nt),
        in_specs=[
            pl.BlockSpec((CONV_TT, CONV_WIDTH), lambda b, t: (b * nt + t, 0)),
            pl.BlockSpec((CONV_K, CONV_WIDTH), lambda b, t: (0, 0)),
            vec, vec, vec,
        ],
        out_specs=(pl.BlockSpec((CONV_TT, CONV_WIDTH), lambda b, t: (b * nt + t, 0)),
                   pl.BlockSpec((1, CONV_K - 1, CONV_WIDTH), lambda b, t: (b, 0, 0))),
        scratch_shapes=[
            pltpu.VMEM((V7X_SUBLANES, CONV_HALO + CONV_TT + V7X_SUBLANES, CONV_WIDTH), F32),
            pltpu.VMEM((CONV_RC, CONV_WIDTH), F32),
            pltpu.VMEM((CONV_K, V7X_SUBLANES, CONV_WIDTH), F32),
        ],
        compiler_params=_params(2),
        name="conv_prompt",
    )(glu, conv_w, conv_b, ln_g, ln_b)


CONV_SB = 16


def _conv_sample_kernel(x_ref, st_ref, w_ref, cb_ref, g_ref, b_ref, o_ref, nst_ref, cbuf_ref):
    nb = CONV_K - 1
    w_old = w_ref[0:nb, :]
    w_new = w_ref[nb:CONV_K, :]
    for s in range(CONV_SB):
        st = st_ref[s]
        x = x_ref[s:s + 1, :]
        cbuf_ref[s:s + 1, :] = jnp.sum(st * w_old, axis=0, keepdims=True) + x * w_new
        nst_ref[s, 0:nb - 1, :] = st_ref[s, 1:nb, :]
        nst_ref[s, nb - 1:nb, :] = x
    o_ref[...] = _ln_swish(cbuf_ref[...] + cb_ref[...], g_ref[...], b_ref[...]).astype(o_ref.dtype)


def conv_sample(glu, state_conv, conv_w, conv_b, ln_g, ln_b):
    vec = pl.BlockSpec((1, CONV_WIDTH), lambda i: (0, 0))
    st_spec = pl.BlockSpec((CONV_SB, CONV_K - 1, CONV_WIDTH), lambda i: (i, 0, 0))
    return pl.pallas_call(
        _conv_sample_kernel,
        out_shape=(jax.ShapeDtypeStruct((S_ROWS, CONV_WIDTH), BF16),
                   jax.ShapeDtypeStruct((S_ROWS, CONV_K - 1, CONV_WIDTH), F32)),
        grid=(S_ROWS // CONV_SB,),
        in_specs=[
            pl.BlockSpec((CONV_SB, CONV_WIDTH), lambda i: (i, 0)),
            st_spec,
            pl.BlockSpec((CONV_K, CONV_WIDTH), lambda i: (0, 0)),
            vec, vec, vec,
        ],
        out_specs=(pl.BlockSpec((CONV_SB, CONV_WIDTH), lambda i: (i, 0)), st_spec),
        scratch_shapes=[pltpu.VMEM((CONV_SB, CONV_WIDTH), F32)],
        compiler_params=_params(1),
        name="conv_sample",
    )(glu, state_conv, conv_w, conv_b, ln_g, ln_b)


S5_GB = 16
S5_CH = S5_GB * SSM_GROUP
S5_ST = S5_GB * SSM_STATE
S5_NB = SSM_GROUPS // S5_GB
S5_SEG = V7X_SUBLANES
S5_SEGLEN = SEQ // S5_SEG
S5_TC = 64
S5_NC = S5_SEGLEN // S5_TC
S5_CR = S5_TC * S5_SEG
S5_LH = S5_CH // V7X_LANES


def _s5_discretize(lr_ref, li_ref, ldt_ref):
    lr, li = lr_ref[0], li_ref[0]
    dt = jnp.exp(ldt_ref[0])
    mag = jnp.exp(lr * dt)
    a_re, a_im = mag * jnp.cos(li * dt), mag * jnp.sin(li * dt)
    den = lr * lr + li * li
    nr, ni = a_re - 1.0, a_im
    f_re = (nr * lr + ni * li) / den
    f_im = (ni * lr - nr * li) / den
    return a_re, a_im, f_re, f_im


def _exact_transpose(x):
    k = x.shape[1]
    eye = jnp.where(lax.broadcasted_iota(jnp.int32, (k, k), 0) == lax.broadcasted_iota(jnp.int32, (k, k), 1),
                    1.0, 0.0).astype(BF16)

    def through_identity(piece):
        return lax.dot_general(eye, piece, (((1,), (1,)), ((), ())), preferred_element_type=F32)

    hi = x.astype(BF16)
    rest = x - hi.astype(F32)
    mid = rest.astype(BF16)
    lo = (rest - mid.astype(F32)).astype(BF16)
    return through_identity(hi) + through_identity(mid) + through_identity(lo)


def _s5_fill_weights(f_re, f_im, bre_ref, bim_ref, cre_ref, cim_ref, bb_ref, cc_ref):
    def over_groups(v):
        return jnp.concatenate([v] * S5_GB, axis=0)

    bt_re, bt_im = _exact_transpose(bre_ref[...]), _exact_transpose(bim_ref[...])
    same = (lax.broadcasted_iota(jnp.int32, (S5_CH, S5_ST), 0) // SSM_GROUP
            == lax.broadcasted_iota(jnp.int32, (S5_CH, S5_ST), 1) // SSM_STATE)
    bb_ref[:, 0:S5_ST] = jnp.where(same, over_groups(f_re * bt_re - f_im * bt_im), 0.0).astype(BF16)
    bb_ref[:, S5_ST:2 * S5_ST] = jnp.where(same, over_groups(f_re * bt_im + f_im * bt_re), 0.0).astype(BF16)

    ct_re, ct_im = _exact_transpose(cre_ref[...]), _exact_transpose(cim_ref[...])
    same = (lax.broadcasted_iota(jnp.int32, (S5_ST, S5_CH), 0) // SSM_STATE
            == lax.broadcasted_iota(jnp.int32, (S5_ST, S5_CH), 1) // SSM_GROUP)
    cc_ref[0:S5_ST, :] = jnp.where(same, over_groups(ct_re), 0.0).astype(BF16)
    cc_ref[S5_ST:2 * S5_ST, :] = jnp.where(same, over_groups(-ct_im), 0.0).astype(BF16)


def _s5_prompt_kernel(u_ref, lr_ref, li_ref, ldt_ref, bdre_ref, bdim_ref, cdre_ref, cdim_ref, d_ref,
                      sg_ref, sgb_ref, hre_ref, him_ref,
                      bb_ref, cc_ref, ul_ref, sgl_ref, lhs_ref, bu_ref, hch_ref):
    a_re, a_im, f_re, f_im = _s5_discretize(lr_ref, li_ref, ldt_ref)

    @pl.when(pl.program_id(1) == 0)
    def _():
        _s5_fill_weights(f_re, f_im, bdre_ref, bdim_ref, cdre_ref, cdim_ref, bb_ref, cc_ref)

    are8 = jnp.broadcast_to(a_re, (S5_SEG, S5_ST))
    aim8 = jnp.broadcast_to(a_im, (S5_SEG, S5_ST))
    d = d_ref[0]
    for hh in range(S5_LH):
        ul_ref[hh] = u_ref[:, hh * V7X_LANES:(hh + 1) * V7X_LANES]

    def chunk_rows(c):
        return slice(c * S5_CR, (c + 1) * S5_CR)

    def project_in(c):
        for tl in range(S5_TC):
            t = c * S5_TC + tl
            for hh in range(S5_LH):
                lhs_ref[t * S5_SEG:(t + 1) * S5_SEG, hh * V7X_LANES:(hh + 1) * V7X_LANES] = (
                    ul_ref[hh, pl.ds(t, S5_SEG, stride=S5_SEGLEN), :])
        bu_ref[chunk_rows(c), :] = _dot(lhs_ref[chunk_rows(c), :].astype(BF16), bb_ref[...])

    def scan(c, hr, hi, keep):
        for tl in range(S5_TC):
            t = c * S5_TC + tl
            rows = slice(t * S5_SEG, (t + 1) * S5_SEG)
            br = bu_ref[rows, 0:S5_ST]
            bi = bu_ref[rows, S5_ST:2 * S5_ST]
            hr, hi = are8 * hr - aim8 * hi + br, are8 * hi + aim8 * hr + bi
            if keep:
                crow = slice(tl * S5_SEG, (tl + 1) * S5_SEG)
                hch_ref[c % 2, crow, 0:S5_ST] = hr
                hch_ref[c % 2, crow, S5_ST:2 * S5_ST] = hi
        return hr, hi

    def project_out(c):
        y = _dot(hch_ref[c % 2].astype(BF16), cc_ref[...]) + d * lhs_ref[chunk_rows(c), :]
        lhs_ref[chunk_rows(c), :] = jax.nn.gelu(y)
        for tl in range(S5_TC):
            t = c * S5_TC + tl
            for hh in range(S5_LH):
                sgl_ref[hh, pl.ds(t, S5_SEG, stride=S5_SEGLEN), :] = (
                    lhs_ref[t * S5_SEG:(t + 1) * S5_SEG, hh * V7X_LANES:(hh + 1) * V7X_LANES])

    zeros = jnp.zeros((S5_SEG, S5_ST), F32)
    e_re, e_im = zeros, zeros
    project_in(0)
    for c in range(S5_NC):
        if c + 1 < S5_NC:
            project_in(c + 1)
        e_re, e_im = scan(c, e_re, e_im, keep=False)

    p_re, p_im = a_re, a_im
    for _ in range(int(math.log2(S5_SEGLEN))):
        p_re, p_im = p_re * p_re - p_im * p_im, 2.0 * p_re * p_im
    seg = lax.broadcasted_iota(jnp.int32, (S5_SEG, S5_ST), 0)
    qr = qi = jnp.zeros((1, S5_ST), F32)
    h_re = h_im = zeros
    for s in range(1, S5_SEG):
        qr, qi = (p_re * qr - p_im * qi + e_re[s - 1:s, :],
                  p_re * qi + p_im * qr + e_im[s - 1:s, :])
        h_re = jnp.where(seg == s, qr, h_re)
        h_im = jnp.where(seg == s, qi, h_im)

    for c in range(S5_NC):
        h_re, h_im = scan(c, h_re, h_im, keep=True)
        if c > 0:
            project_out(c - 1)
    project_out(S5_NC - 1)

    hre_ref[0] = h_re[S5_SEG - 1:S5_SEG, :]
    him_ref[0] = h_im[S5_SEG - 1:S5_SEG, :]
    for hh in range(S5_LH):
        sg_ref[:, hh * V7X_LANES:(hh + 1) * V7X_LANES] = sgl_ref[hh]
        sgb_ref[:, hh * V7X_LANES:(hh + 1) * V7X_LANES] = sgl_ref[hh].astype(BF16)


def _s5_param_specs(idx):
    def vec(width):
        return pl.BlockSpec((1, 1, width), lambda *g: (idx(*g), 0, 0))

    def rows(shape):
        return pl.BlockSpec(shape, lambda *g: (idx(*g), 0))

    return [vec(S5_ST), vec(S5_ST), vec(S5_ST),
            rows((S5_ST, SSM_GROUP)), rows((S5_ST, SSM_GROUP)),
            rows((S5_CH, SSM_STATE)), rows((S5_CH, SSM_STATE)), vec(S5_CH)]


def s5_prompt(u, s5p):
    return pl.pallas_call(
        _s5_prompt_kernel,
        out_shape=(jax.ShapeDtypeStruct((P_ROWS, SSM_WIDTH), F32),
                   jax.ShapeDtypeStruct((P_ROWS, SSM_WIDTH), BF16),
                   jax.ShapeDtypeStruct((BATCH, 1, SSM_GROUPS * SSM_STATE), F32),
                   jax.ShapeDtypeStruct((BATCH, 1, SSM_GROUPS * SSM_STATE), F32)),
        grid=(S5_NB, BATCH),
        in_specs=[pl.BlockSpec((SEQ, S5_CH), lambda j, b: (b, j))] + _s5_param_specs(lambda j, b: j),
        out_specs=(pl.BlockSpec((SEQ, S5_CH), lambda j, b: (b, j)),
                   pl.BlockSpec((SEQ, S5_CH), lambda j, b: (b, j)),
                   pl.BlockSpec((1, 1, S5_ST), lambda j, b: (b, 0, j)),
                   pl.BlockSpec((1, 1, S5_ST), lambda j, b: (b, 0, j))),
        scratch_shapes=[
            pltpu.VMEM((S5_CH, 2 * S5_ST), BF16),
            pltpu.VMEM((2 * S5_ST, S5_CH), BF16),
            pltpu.VMEM((S5_LH, SEQ, V7X_LANES), F32),
            pltpu.VMEM((S5_LH, SEQ, V7X_LANES), F32),
            pltpu.VMEM((SEQ, S5_CH), F32),
            pltpu.VMEM((SEQ, 2 * S5_ST), F32),
            pltpu.VMEM((2, S5_CR, 2 * S5_ST), F32),
        ],
        compiler_params=_params(2),
        name="s5_prompt",
    )(u, *s5p)


def _s5_sample_kernel(u_ref, h0re_ref, h0im_ref, lr_ref, li_ref, ldt_ref, bdre_ref, bdim_ref,
                      cdre_ref, cdim_ref, d_ref, sg_ref, sgb_ref, hre_ref, him_ref, bb_ref, cc_ref):
    a_re, a_im, f_re, f_im = _s5_discretize(lr_ref, li_ref, ldt_ref)
    _s5_fill_weights(f_re, f_im, bdre_ref, bdim_ref, cdre_ref, cdim_ref, bb_ref, cc_ref)
    u = u_ref[...]
    bu = _dot(u.astype(BF16), bb_ref[...])
    h0r, h0i = h0re_ref[...], h0im_ref[...]
    hr = a_re * h0r - a_im * h0i + bu[:, 0:S5_ST]
    hi = a_re * h0i + a_im * h0r + bu[:, S5_ST:2 * S5_ST]
    hre_ref[...] = hr
    him_ref[...] = hi
    y = _dot(hr.astype(BF16), cc_ref[0:S5_ST, :]) + _dot(hi.astype(BF16), cc_ref[S5_ST:2 * S5_ST, :])
    sg = jax.nn.gelu(y + d_ref[0] * u)
    sg_ref[...] = sg
    sgb_ref[...] = sg.astype(BF16)


def s5_sample(u, h0_re, h0_im, s5p):
    st = pl.BlockSpec((S_ROWS, S5_ST), lambda j: (0, j))
    return pl.pallas_call(
        _s5_sample_kernel,
        out_shape=(jax.ShapeDtypeStruct((S_ROWS, SSM_WIDTH), F32),
                   jax.ShapeDtypeStruct((S_ROWS, SSM_WIDTH), BF16),
                   jax.ShapeDtypeStruct((S_ROWS, SSM_GROUPS * SSM_STATE), F32),
                   jax.ShapeDtypeStruct((S_ROWS, SSM_GROUPS * SSM_STATE), F32)),
        grid=(S5_NB,),
        in_specs=[pl.BlockSpec((S_ROWS, S5_CH), lambda j: (0, j)), st, st]
        + _s5_param_specs(lambda j: j),
        out_specs=(pl.BlockSpec((S_ROWS, S5_CH), lambda j: (0, j)),
                   pl.BlockSpec((S_ROWS, S5_CH), lambda j: (0, j)), st, st),
        scratch_shapes=[
            pltpu.VMEM((S5_CH, 2 * S5_ST), BF16),
            pltpu.VMEM((2 * S5_ST, S5_CH), BF16),
        ],
        compiler_params=_params(1),
        name="s5_sample",
    )(u, h0_re, h0_im, *s5p)


def s5_block_params(lam_re, lam_im, log_dt, b_re, b_im, c_re, c_im, d_skip):
    n_state = SSM_GROUPS * SSM_STATE
    return (lam_re.reshape(S5_NB, 1, S5_ST), lam_im.reshape(S5_NB, 1, S5_ST),
            jnp.repeat(log_dt, SSM_STATE).reshape(S5_NB, 1, S5_ST),
            b_re.reshape(n_state, SSM_GROUP), b_im.reshape(n_state, SSM_GROUP),
            c_re.reshape(SSM_WIDTH, SSM_STATE), c_im.reshape(SSM_WIDTH, SSM_STATE),
            d_skip.reshape(S5_NB, 1, S5_CH))


SSM_GLU_TN = 512


def _ssm_glu_kernel(ap_hbm, as_ref, w_ref, tp_ref, ts_ref, op_ref, os_ref, wc_ref, r_ref, ap_ref, sem):
    tile = _RowTile([ap_hbm], [ap_ref], [sem], SSM_WIDTH // SSM_GLU_TN)

    def body(col):
        wc_ref[...] = w_ref[...].astype(BF16)

        def mm(k):
            r_ref[_sub_rows(k), :] = _dot(ap_ref[_sub_rows(k), :], wc_ref[...])

        def epi(k):
            op_ref[_sub_rows(k), :] = (
                tp_ref[_sub_rows(k), :] * jax.nn.sigmoid(r_ref[_sub_rows(k), :])).astype(op_ref.dtype)

        _interleave(NSUB, tile.mm_hooks(col, mm), epi)

        @_on_last_row_tile
        def _():
            z = _dot(as_ref[...], wc_ref[...])
            os_ref[...] = (ts_ref[...] * jax.nn.sigmoid(z)).astype(os_ref.dtype)

    tile.run(body)


def ssm_glu(sg_p, sg_s, sgb_p, sgb_s, w_glu):
    tn = SSM_GLU_TN
    return pl.pallas_call(
        _ssm_glu_kernel,
        out_shape=(jax.ShapeDtypeStruct((P_ROWS, SSM_WIDTH), BF16),
                   jax.ShapeDtypeStruct((S_ROWS, SSM_WIDTH), BF16)),
        grid=(NPT, SSM_WIDTH // tn),
        in_specs=[
            _ANY,
            _resident((S_ROWS, SSM_WIDTH), lambda i, j: (0, 0)),
            pl.BlockSpec((SSM_WIDTH, tn), lambda i, j: (0, j)),
            pl.BlockSpec((TM, tn), lambda i, j: (i, j)),
            pl.BlockSpec((S_ROWS, tn), lambda i, j: (0, j)),
        ],
        out_specs=(pl.BlockSpec((TM, tn), lambda i, j: (i, j)),
                   pl.BlockSpec((S_ROWS, tn), _sample_cols)),
        scratch_shapes=[pltpu.VMEM((SSM_WIDTH, tn), BF16), pltpu.VMEM((TM, tn), F32)]
        + _row_tile_scratch([SSM_WIDTH]),
        compiler_params=_params(2),
        name="ssm_glu",
    )(sgb_p, sgb_s, w_glu, sg_p, sg_s)


MERGE_TN = 256


def _merge_kernel(cp_hbm, cs_ref, yp_hbm, ys_ref, wa_ref, wb_ref, gap_ref, gbp_ref, gas_ref, gbs_ref,
                  op_ref, os_ref, wc_ref, r_ref, cp_ref, yp_ref, csem, ysem):
    tile = _RowTile([cp_hbm, yp_hbm], [cp_ref, yp_ref], [csem, ysem], D_MODEL // MERGE_TN)

    def body(col):
        wc_ref[0] = wa_ref[...].astype(BF16)
        wc_ref[1] = wb_ref[...].astype(BF16)

        def mm(k):
            r_ref[0, _sub_rows(k), :] = _dot(cp_ref[_sub_rows(k), :], wc_ref[0])
            r_ref[1, _sub_rows(k), :] = _dot(yp_ref[_sub_rows(k), :], wc_ref[1])

        def epi(k):
            rows = _sub_rows(k)
            op_ref[rows, :] = (gap_ref[rows, :] * r_ref[0, rows, :]
                               + gbp_ref[rows, :] * r_ref[1, rows, :]).astype(op_ref.dtype)

        _interleave(NSUB, tile.mm_hooks(col, mm), epi)

        @_on_last_row_tile
        def _():
            ya = _dot(cs_ref[...], wc_ref[0])
            yb = _dot(ys_ref[...], wc_ref[1])
            os_ref[...] = (gas_ref[...] * ya + gbs_ref[...] * yb).astype(os_ref.dtype)

    tile.run(body)


def merge(c_p, c_s, yg_p, yg_s, w_conv_out, w_ssm_out, gates_p, gates_s):
    tn = MERGE_TN
    nj = D_MODEL // tn
    return pl.pallas_call(
        _merge_kernel,
        out_shape=(jax.ShapeDtypeStruct((P_ROWS, D_MODEL), BF16),
                   jax.ShapeDtypeStruct((S_ROWS, D_MODEL), BF16)),
        grid=(NPT, nj),
        in_specs=[
            _ANY,
            _resident((S_ROWS, CONV_WIDTH), lambda i, j: (0, 0)),
            _ANY,
            _resident((S_ROWS, SSM_WIDTH), lambda i, j: (0, 0)),
            pl.BlockSpec((CONV_WIDTH, tn), lambda i, j: (0, j)),
            pl.BlockSpec((SSM_WIDTH, tn), lambda i, j: (0, j)),
            pl.BlockSpec((TM, tn), lambda i, j: (i, j)),
            pl.BlockSpec((TM, tn), lambda i, j: (i, j + nj)),
            pl.BlockSpec((S_ROWS, tn), lambda i, j: (0, j)),
            pl.BlockSpec((S_ROWS, tn), lambda i, j: (0, j + nj)),
        ],
        out_specs=(pl.BlockSpec((TM, tn), lambda i, j: (i, j)),
                   pl.BlockSpec((S_ROWS, tn), _sample_cols)),
        scratch_shapes=[pltpu.VMEM((2, CONV_WIDTH, tn), BF16), pltpu.VMEM((2, TM, tn), F32)]
        + _row_tile_scratch([CONV_WIDTH, SSM_WIDTH]),
        compiler_params=_params(2),
        name="merge",
    )(c_p, c_s, yg_p, yg_s, w_conv_out, w_ssm_out, gates_p, gates_p, gates_s, gates_s)


OPROJ_TN = 512


def _oproj_kernel(ap_hbm, as_ref, w_ref, xp_ref, xs_ref, op_ref, os_ref, wc_ref, ap_ref, sem):
    tile = _RowTile([ap_hbm], [ap_ref], [sem], D_MODEL // OPROJ_TN)

    def body(col):
        wc_ref[...] = w_ref[...].astype(BF16)

        def mm(k):
            op_ref[_sub_rows(k), :] = _dot(ap_ref[_sub_rows(k), :], wc_ref[...])

        def epi(k):
            op_ref[_sub_rows(k), :] = op_ref[_sub_rows(k), :] + xp_ref[_sub_rows(k), :]

        _interleave(NSUB, tile.mm_hooks(col, mm), epi)

        @_on_last_row_tile
        def _():
            os_ref[...] = xs_ref[...] + _dot(as_ref[...], wc_ref[...])

    tile.run(body)


def oproj(merged_p, merged_s, w_o, xp, xs):
    tn = OPROJ_TN
    return pl.pallas_call(
        _oproj_kernel,
        out_shape=(jax.ShapeDtypeStruct((P_ROWS, D_MODEL), F32),
                   jax.ShapeDtypeStruct((S_ROWS, D_MODEL), F32)),
        grid=(NPT, D_MODEL // tn),
        in_specs=[
            _ANY,
            _resident((S_ROWS, D_MODEL), lambda i, j: (0, 0)),
            pl.BlockSpec((D_MODEL, tn), lambda i, j: (0, j)),
            pl.BlockSpec((TM, tn), lambda i, j: (i, j)),
            pl.BlockSpec((S_ROWS, tn), lambda i, j: (0, j)),
        ],
        out_specs=(pl.BlockSpec((TM, tn), lambda i, j: (i, j)),
                   pl.BlockSpec((S_ROWS, tn), _sample_cols)),
        scratch_shapes=[pltpu.VMEM((D_MODEL, tn), BF16)] + _row_tile_scratch([D_MODEL]),
        compiler_params=_params(2),
        name="oproj",
    )(merged_p, merged_s, w_o, xp, xs)


DOWN_TM = 512
DOWN_NPT = P_ROWS // DOWN_TM


def _ffn_down_kernel(ap_ref, as_ref, w_ref, xp_ref, xs_ref, op_ref, os_ref):
    op_ref[...] = xp_ref[...] + _dot(ap_ref[...], w_ref[...])

    @pl.when(pl.program_id(0) == DOWN_NPT - 1)
    def _():
        os_ref[...] = xs_ref[...] + _dot(as_ref[...], w_ref[...])


def ffn_down(h_p, h_s, w_down_bf16, x1_p, x1_s):
    tn = 512
    last = DOWN_NPT - 1
    return pl.pallas_call(
        _ffn_down_kernel,
        out_shape=(jax.ShapeDtypeStruct((P_ROWS, D_MODEL), F32),
                   jax.ShapeDtypeStruct((S_ROWS, D_MODEL), F32)),
        grid=(DOWN_NPT, D_MODEL // tn),
        in_specs=[
            pl.BlockSpec((DOWN_TM, D_FF), lambda i, j: (i, 0)),
            _resident((S_ROWS, D_FF), lambda i, j: (0, 0)),
            pl.BlockSpec((D_FF, tn), lambda i, j: (0, j)),
            pl.BlockSpec((DOWN_TM, tn), lambda i, j: (i, j)),
            pl.BlockSpec((S_ROWS, tn), lambda i, j: (0, j)),
        ],
        out_specs=(pl.BlockSpec((DOWN_TM, tn), lambda i, j: (i, j)),
                   pl.BlockSpec((S_ROWS, tn), lambda i, j: (0, jnp.where(i == last, j, 0)))),
        compiler_params=_params(2),
        name="ffn_down",
    )(h_p, h_s, w_down_bf16, x1_p, x1_s)


FFN_TN = 256
FFN_NJ = D_FF // FFN_TN
FFN_PAD = V7X_SUBLANES


def _ffn_up_kernel(ap_hbm, as_ref, wg_ref, wv_ref, s0_ref, s1_ref, cw_ref, cb_ref, wd_ref,
                   hp_ref, hs_ref, tail_ref, gs_ref, wdb_ref, wc_ref, g_ref, v_ref, ap_ref, sem):
    tn = FFN_TN
    tile = _RowTile([ap_hbm], [ap_ref], [sem], FFN_NJ)

    def body(col):
        wc_ref[:, 0:tn] = wg_ref[...].astype(BF16)
        wc_ref[:, tn:2 * tn] = wv_ref[...].astype(BF16)
        wdb_ref[...] = wd_ref[...].astype(BF16)
        w0, w1, w2 = cw_ref[0:1, :], cw_ref[1:2, :], cw_ref[2:3, :]
        cb = cb_ref[...]
        g_ref[0:FFN_PAD, :] = jnp.zeros((FFN_PAD, tn), F32)

        def mm(k):
            r = _dot(ap_ref[_sub_rows(k), :], wc_ref[...])
            g_ref[FFN_PAD + k * SUB:FFN_PAD + (k + 1) * SUB, :] = r[:, 0:tn]
            v_ref[_sub_rows(k), :] = r[:, tn:2 * tn]

        def epi(k):
            r0 = FFN_PAD + k * SUB
            gc = (w0 * g_ref[r0 - 2:r0 - 2 + SUB, :] + w1 * g_ref[r0 - 1:r0 - 1 + SUB, :]
                  + w2 * g_ref[r0:r0 + SUB, :] + cb)
            hp_ref[_sub_rows(k), :] = (gc * jax.nn.sigmoid(gc) * v_ref[_sub_rows(k), :]).astype(hp_ref.dtype)

        _interleave(NSUB, tile.mm_hooks(col, mm), epi)
        tail_ref[0] = g_ref[TM:TM + FFN_PAD, :]

        @_on_last_row_tile
        def _():
            r = _dot(as_ref[...], wc_ref[...])
            gate, val = r[:, 0:tn], r[:, tn:2 * tn]
            gc = w0 * s0_ref[...] + w1 * s1_ref[...] + w2 * gate + cb
            hs_ref[...] = (gc * jax.nn.sigmoid(gc) * val).astype(hs_ref.dtype)
            gs_ref[...] = gate

    tile.run(body)


def ffn_up(xn2_p, xn2_s, w_up, ffn_old, ffn_conv_w, ffn_conv_b, w_down):
    nj = FFN_NJ
    tn = FFN_TN
    wd_rows = D_FF // nj

    def wd_slab(i, j):
        return (jnp.where(i == 0, j, nj - 1), 0)

    return pl.pallas_call(
        _ffn_up_kernel,
        out_shape=(jax.ShapeDtypeStruct((P_ROWS, D_FF), BF16),
                   jax.ShapeDtypeStruct((S_ROWS, D_FF), BF16),
                   jax.ShapeDtypeStruct((NPT, FFN_PAD, D_FF), F32),
                   jax.ShapeDtypeStruct((S_ROWS, D_FF), F32),
                   jax.ShapeDtypeStruct((D_FF, D_MODEL), BF16)),
        grid=(NPT, nj),
        in_specs=[
            _ANY,
            _resident((S_ROWS, D_MODEL), lambda i, j: (0, 0)),
            pl.BlockSpec((D_MODEL, tn), lambda i, j: (0, j)),
            pl.BlockSpec((D_MODEL, tn), lambda i, j: (0, j + nj)),
            pl.BlockSpec((S_ROWS, tn), lambda i, j: (0, j)),
            pl.BlockSpec((S_ROWS, tn), lambda i, j: (0, j)),
            pl.BlockSpec((FFN_K, tn), lambda i, j: (0, j)),
            pl.BlockSpec((1, tn), lambda i, j: (0, j)),
            pl.BlockSpec((wd_rows, D_MODEL), wd_slab),
        ],
        out_specs=(
            pl.BlockSpec((TM, tn), lambda i, j: (i, j)),
            pl.BlockSpec((S_ROWS, tn), _sample_cols),
            pl.BlockSpec((1, FFN_PAD, tn), lambda i, j: (i, 0, j)),
            pl.BlockSpec((S_ROWS, tn), _sample_cols),
            pl.BlockSpec((wd_rows, D_MODEL), wd_slab),
        ),
        scratch_shapes=[
            pltpu.VMEM((D_MODEL, 2 * tn), BF16),
            pltpu.VMEM((FFN_PAD + TM, tn), F32),
            pltpu.VMEM((TM, tn), F32),
        ] + _row_tile_scratch([D_MODEL]),
        compiler_params=_params(2),
        name="ffn_up",
    )(xn2_p, xn2_s, w_up, w_up, ffn_old[0], ffn_old[1], ffn_conv_w, ffn_conv_b, w_down)


def kernel(x_prompt, x_sample, state_conv, state_ssm_re, state_ssm_im, state_ffn_conv,
           norm_mix_g, w_in, conv_w, conv_b, ln_g, ln_b, w_conv_out,
           lam_re, lam_im, log_dt, b_re, b_im, c_re, c_im, d_skip, w_glu, w_ssm_out, w_o,
           norm_ffn_g, w_up, ffn_conv_w, ffn_conv_b, w_down, final_norm_g):
    xp = x_prompt.reshape(P_ROWS, D_MODEL)
    xs = x_sample.reshape(S_ROWS, D_MODEL)

    def row(v):
        return v.reshape(1, -1)

    xn_p, xn_s = rownorm_pair(xp, xs, row(norm_mix_g[0]), BF16)
    glu_p, glu_s = inproj_glu(xn_p, xn_s, w_in[0])
    u_p, u_s = inproj_cols(xn_p, xn_s, w_in[0], 2 * CONV_WIDTH, SSM_WIDTH, False, "inproj_ssm")
    gates_p, gates_s = inproj_cols(xn_p, xn_s, w_in[0], 2 * CONV_WIDTH + SSM_WIDTH, 2 * D_MODEL, True,
                                   "inproj_gates")

    conv_vecs = (conv_w[0], row(conv_b[0]), row(ln_g[0]), row(ln_b[0]))
    c_p, conv_p = conv_prompt(glu_p, *conv_vecs)
    c_s, conv_s = conv_sample(glu_s, state_conv[0], *conv_vecs)

    s5p = s5_block_params(lam_re[0], lam_im[0], log_dt[0], b_re[0], b_im[0], c_re[0], c_im[0], d_skip[0])
    n_state = SSM_GROUPS * SSM_STATE
    sg_p, sgb_p, ssr_p, ssi_p = s5_prompt(u_p, s5p)
    sg_s, sgb_s, ssr_s, ssi_s = s5_sample(u_s, state_ssm_re[0].reshape(S_ROWS, n_state),
                                          state_ssm_im[0].reshape(S_ROWS, n_state), s5p)
    yg_p, yg_s = ssm_glu(sg_p, sg_s, sgb_p, sgb_s, w_glu[0])

    merged_p, merged_s = merge(c_p, c_s, yg_p, yg_s, w_conv_out[0], w_ssm_out[0], gates_p, gates_s)
    x1_p, x1_s = oproj(merged_p, merged_s, w_o[0], xp, xs)

    xn2_p, xn2_s = rownorm_pair(x1_p, x1_s, row(norm_ffn_g[0]), BF16)
    ffn_old = [state_ffn_conv[0, :, k, :] for k in range(FFN_K - 1)]
    h_p, h_s, gate_tail, gate_s, w_down_bf16 = ffn_up(
        xn2_p, xn2_s, w_up[0], ffn_old, ffn_conv_w[0], row(ffn_conv_b[0]), w_down[0])
    x2_p, x2_s = ffn_down(h_p, h_s, w_down_bf16, x1_p, x1_s)
    y_p, y_s = rownorm_pair(x2_p, x2_s, row(final_norm_g), F32)

    ffn_p = gate_tail[:, FFN_PAD - (FFN_K - 1):, :]
    ffn_s = jnp.stack([ffn_old[1], gate_s], axis=1)
    state_shape = (1, -1, SSM_GROUPS, SSM_STATE)
    return (y_p.reshape(BATCH, SEQ, D_MODEL), y_s.reshape(DEC_BATCH, 1, D_MODEL),
            conv_p[None], conv_s[None],
            ssr_p.reshape(state_shape), ssi_p.reshape(state_shape),
            ssr_s.reshape(state_shape), ssi_s.reshape(state_shape),
            ffn_p[None], ffn_s[None])
```

```python
import functools
import math

import jax
import jax.numpy as jnp
from jax import lax
from jax.experimental import pallas as pl
from jax.experimental.pallas import tpu as pltpu

D_MODEL = 4096
BATCH = 4
SEQ = 2048
DEC_BATCH = 128
CONV_WIDTH = D_MODEL // 2
CONV_K = 31
SSM_WIDTH = D_MODEL // 2
SSM_GROUP = 16
SSM_GROUPS = SSM_WIDTH // SSM_GROUP
SSM_STATE = 64
D_FF = 11008
FFN_K = 3
EPS = 1e-6

P_ROWS = BATCH * SEQ
S_ROWS = DEC_BATCH

V7X_SUBLANES = 8
V7X_LANES = 128
V7X_VMEM_LIMIT = 58 * 1024 * 1024

TM = SEQ
NPT = P_ROWS // TM
SUB_SIZES = (768, 768, 384, 128)
SUB_STARTS = tuple(sum(SUB_SIZES[:k]) for k in range(len(SUB_SIZES)))
NSUB = len(SUB_SIZES)
assert sum(SUB_SIZES) == TM

BF16 = jnp.bfloat16
F32 = jnp.float32


def _params(n_grid_dims):
    return pltpu.CompilerParams(
        dimension_semantics=("arbitrary",) * n_grid_dims,
        vmem_limit_bytes=V7X_VMEM_LIMIT)


def _dot(a, b):
    return jnp.dot(a, b, preferred_element_type=F32)


def _rms(x, g):
    return x * lax.rsqrt(jnp.mean(x * x, axis=-1, keepdims=True) + EPS) * g


def _interleave(n, mm, epi):
    mm(0)
    for k in range(1, n):
        mm(k)
        epi(k - 1)
    epi(n - 1)


def _sub_rows(k, offset=0):
    return slice(offset + SUB_STARTS[k], offset + SUB_STARTS[k] + SUB_SIZES[k])


class _RowTile:
    def __init__(self, hbm_refs, bufs, sems, ncol):
        self.hbm_refs, self.bufs, self.sems, self.ncol = hbm_refs, bufs, sems, ncol
        self.i, self.j = pl.program_id(0), pl.program_id(1)

    def _copies(self, tile, k):
        return [pltpu.make_async_copy(hbm.at[pl.ds(tile * TM + SUB_STARTS[k], SUB_SIZES[k]), :],
                                      buf.at[pl.ds(SUB_STARTS[k], SUB_SIZES[k]), :], sem.at[k])
                for hbm, buf, sem in zip(self.hbm_refs, self.bufs, self.sems)]

    def fetch_first_tile(self):
        @pl.when((self.i == 0) & (self.j == 0))
        def _():
            for k in range(NSUB):
                for c in self._copies(0, k):
                    c.start()

    def wait(self, k):
        for c in self._copies(self.i, k):
            c.wait()

    def fetch_next(self, k):
        @pl.when(self.i < NPT - 1)
        def _():
            for c in self._copies(self.i + 1, k):
                c.start()

    def run(self, body):
        assert self.ncol >= 2
        self.fetch_first_tile()
        pl.when(self.j == 0)(functools.partial(body, "first"))
        pl.when((self.j > 0) & (self.j < self.ncol - 1))(functools.partial(body, "mid"))
        pl.when(self.j == self.ncol - 1)(functools.partial(body, "last"))

    def mm_hooks(self, col, mm):
        def wrapped(k):
            if col == "first":
                self.wait(k)
            mm(k)
            if col == "last":
                self.fetch_next(k)
        return wrapped


_ANY = pl.BlockSpec(memory_space=pl.ANY)


def _row_tile_scratch(widths):
    return ([pltpu.VMEM((TM, w), BF16) for w in widths]
            + [pltpu.SemaphoreType.DMA((NSUB,)) for _ in widths])


def _resident(shape, index_map):
    return pl.BlockSpec(shape, index_map, pipeline_mode=pl.Buffered(1))


def _sample_cols(i, j):
    return (0, jnp.where(i == NPT - 1, j, 0))


def _on_last_row_tile(fn):
    pl.when(pl.program_id(0) == NPT - 1)(fn)


def _rownorm_kernel(x_ref, g_ref, o_ref):
    o_ref[...] = _rms(x_ref[...], g_ref[...]).astype(o_ref.dtype)


def rownorm(x, g, out_dtype, row_block):
    rows = x.shape[0]
    return pl.pallas_call(
        _rownorm_kernel,
        out_shape=jax.ShapeDtypeStruct((rows, D_MODEL), out_dtype),
        grid=(rows // row_block,),
        in_specs=[
            pl.BlockSpec((row_block, D_MODEL), lambda i: (i, 0)),
            pl.BlockSpec((1, D_MODEL), lambda i: (0, 0)),
        ],
        out_specs=pl.BlockSpec((row_block, D_MODEL), lambda i: (i, 0)),
        compiler_params=_params(1),
        name="rownorm",
    )(x, g)


def rownorm_pair(xp, xs, g, out_dtype):
    return rownorm(xp, g, out_dtype, 512), rownorm(xs, g, out_dtype, S_ROWS)


GLU_TN = 256


def _inproj_glu_kernel(ap_hbm, as_ref, wa_ref, wg_ref, op_ref, os_ref, wc_ref, r_ref, ap_ref, sem):
    tn = GLU_TN
    tile = _RowTile([ap_hbm], [ap_ref], [sem], CONV_WIDTH // tn)

    def body(col):
        wc_ref[:, 0:tn] = wa_ref[...].astype(BF16)
        wc_ref[:, tn:2 * tn] = wg_ref[...].astype(BF16)

        def mm(k):
            r_ref[_sub_rows(k), :] = _dot(ap_ref[_sub_rows(k), :], wc_ref[...])

        def epi(k):
            r = r_ref[_sub_rows(k), :]
            op_ref[_sub_rows(k), :] = r[:, 0:tn] * jax.nn.sigmoid(r[:, tn:2 * tn])

        _interleave(NSUB, tile.mm_hooks(col, mm), epi)

        @_on_last_row_tile
        def _():
            r = _dot(as_ref[...], wc_ref[...])
            os_ref[...] = r[:, 0:tn] * jax.nn.sigmoid(r[:, tn:2 * tn])

    tile.run(body)


def inproj_glu(xn_p, xn_s, w_in):
    tn = GLU_TN
    nj = CONV_WIDTH // tn
    return pl.pallas_call(
        _inproj_glu_kernel,
        out_shape=(jax.ShapeDtypeStruct((P_ROWS, CONV_WIDTH), F32),
                   jax.ShapeDtypeStruct((S_ROWS, CONV_WIDTH), F32)),
        grid=(NPT, nj),
        in_specs=[
            _ANY,
            _resident((S_ROWS, D_MODEL), lambda i, j: (0, 0)),
            pl.BlockSpec((D_MODEL, tn), lambda i, j: (0, j)),
            pl.BlockSpec((D_MODEL, tn), lambda i, j: (0, j + nj)),
        ],
        out_specs=(pl.BlockSpec((TM, tn), lambda i, j: (i, j)),
                   pl.BlockSpec((S_ROWS, tn), _sample_cols)),
        scratch_shapes=[pltpu.VMEM((D_MODEL, 2 * tn), BF16), pltpu.VMEM((TM, 2 * tn), F32)]
        + _row_tile_scratch([D_MODEL]),
        compiler_params=_params(2),
        name="inproj_glu",
    )(xn_p, xn_s, w_in, w_in)


def _inproj_cols_kernel(ap_hbm, as_ref, w_ref, op_ref, os_ref, wc_ref, ap_ref, sem, *, squash, ncol):
    tile = _RowTile([ap_hbm], [ap_ref], [sem], ncol)

    def body(col):
        wc_ref[...] = w_ref[...].astype(BF16)

        def mm(k):
            op_ref[_sub_rows(k), :] = _dot(ap_ref[_sub_rows(k), :], wc_ref[...])

        def epi(k):
            if squash:
                op_ref[_sub_rows(k), :] = jax.nn.sigmoid(op_ref[_sub_rows(k), :])

        _interleave(NSUB, tile.mm_hooks(col, mm), epi)

        @_on_last_row_tile
        def _():
            r = _dot(as_ref[...], wc_ref[...])
            os_ref[...] = jax.nn.sigmoid(r) if squash else r

    tile.run(body)


def inproj_cols(xn_p, xn_s, w_in, col0, width, squash, name):
    tn = 512
    return pl.pallas_call(
        functools.partial(_inproj_cols_kernel, squash=squash, ncol=width // tn),
        out_shape=(jax.ShapeDtypeStruct((P_ROWS, width), F32),
                   jax.ShapeDtypeStruct((S_ROWS, width), F32)),
        grid=(NPT, width // tn),
        in_specs=[
            _ANY,
            _resident((S_ROWS, D_MODEL), lambda i, j: (0, 0)),
            pl.BlockSpec((D_MODEL, tn), lambda i, j: (0, j + col0 // tn)),
        ],
        out_specs=(pl.BlockSpec((TM, tn), lambda i, j: (i, j)),
                   pl.BlockSpec((S_ROWS, tn), _sample_cols)),
        scratch_shapes=[pltpu.VMEM((D_MODEL, tn), BF16)] + _row_tile_scratch([D_MODEL]),
        compiler_params=_params(2),
        name=name,
    )(xn_p, xn_s, w_in)


CONV_TT = 256
CONV_HALO = 32
CONV_RC = 32
CONV_LC = 256
CONV_PIECE = 32


def _ln_swish(c, g, b):
    mu = jnp.mean(c, axis=-1, keepdims=True)
    d = c - mu
    var = jnp.mean(d * d, axis=-1, keepdims=True)
    r = d * lax.rsqrt(var + EPS) * g + b
    return r * jax.nn.sigmoid(r)


def _conv_prompt_kernel(x_ref, w_ref, cb_ref, g_ref, b_ref, o_ref, st_ref, xs_ref, cbuf_ref, w8_ref):
    t = pl.program_id(1)
    nt = pl.num_programs(1)
    sub = V7X_SUBLANES

    @pl.when(t == 0)
    def _():
        xs_ref[0, 0:CONV_HALO, :] = jnp.zeros((CONV_HALO, CONV_WIDTH), F32)
        xs_ref[0, CONV_HALO + CONV_TT:CONV_HALO + CONV_TT + sub, :] = jnp.zeros((sub, CONV_WIDTH), F32)
        for k in range(CONV_K):
            w8_ref[k] = jnp.broadcast_to(w_ref[k:k + 1, :], (sub, CONV_WIDTH))

    xs_ref[0, CONV_HALO:CONV_HALO + CONV_TT, :] = x_ref[...]

    def shift_piece(p, carry):
        r = pl.multiple_of(p * CONV_PIECE, CONV_PIECE)
        piece = xs_ref[0, pl.ds(r, CONV_PIECE + sub), :]
        for m in range(1, sub):
            rolled = pltpu.roll(piece, CONV_PIECE + sub - m, axis=0)
            xs_ref[m, pl.ds(r, CONV_PIECE), :] = rolled[0:CONV_PIECE]
        return carry

    lax.fori_loop(0, (CONV_HALO + CONV_TT) // CONV_PIECE, shift_piece, 0)

    off = CONV_HALO - (CONV_K - 1)

    lane_chunks = [slice(l0, l0 + CONV_LC) for l0 in range(0, CONV_WIDTH, CONV_LC)]

    def chunk(c, carry):
        r0 = pl.multiple_of(c * CONV_RC, CONV_RC)
        part = jnp.zeros((CONV_RC, CONV_LC), F32)
        for lanes in lane_chunks:
            accs = [jnp.zeros((sub, CONV_LC), F32) for _ in range(CONV_RC // sub)]
            for k in range(CONV_K):
                o = off + k
                wk = w8_ref[k, :, lanes]
                for rg in range(CONV_RC // sub):
                    win = xs_ref[o % sub, pl.ds(r0 + (o // sub + rg) * sub, sub), lanes]
                    accs[rg] = accs[rg] + win * wk
            acc = jnp.concatenate(accs, axis=0) + cb_ref[:, lanes]
            cbuf_ref[:, lanes] = acc
            part = part + acc
        mu = jnp.sum(part, axis=-1, keepdims=True) / CONV_WIDTH
        part = jnp.zeros((CONV_RC, CONV_LC), F32)
        for lanes in lane_chunks:
            d = cbuf_ref[:, lanes] - mu
            part = part + d * d
        rstd = lax.rsqrt(jnp.sum(part, axis=-1, keepdims=True) / CONV_WIDTH + EPS)
        for lanes in lane_chunks:
            r = (cbuf_ref[:, lanes] - mu) * rstd * g_ref[:, lanes] + b_ref[:, lanes]
            o_ref[pl.ds(r0, CONV_RC), lanes] = (r * jax.nn.sigmoid(r)).astype(o_ref.dtype)
        return carry

    lax.fori_loop(0, CONV_TT // CONV_RC, chunk, 0)

    @pl.when(t == nt - 1)
    def _():
        st_ref[0] = xs_ref[off, CONV_TT:CONV_TT + CONV_K - 1, :]

    xs_ref[0, 0:CONV_HALO, :] = xs_ref[0, CONV_TT:CONV_TT + CONV_HALO, :]


def conv_prompt(glu, conv_w, conv_b, ln_g, ln_b):
    nt = SEQ // CONV_TT
    vec = pl.BlockSpec((1, CONV_WIDTH), lambda b, t: (0, 0))
    return pl.pallas_call(
        _conv_prompt_kernel,
        out_shape=(jax.ShapeDtypeStruct((P_ROWS, CONV_WIDTH), BF16),
                   jax.ShapeDtypeStruct((BATCH, CONV_K - 1, CONV_WIDTH), F32)),
        grid=(BATCH, nt),
        in_specs=[
            pl.BlockSpec((CONV_TT, CONV_WIDTH), lambda b, t: (b * nt + t, 0)),
            pl.BlockSpec((CONV_K, CONV_WIDTH), lambda b, t: (0, 0)),
            vec, vec, vec,
        ],
        out_specs=(pl.BlockSpec((CONV_TT, CONV_WIDTH), lambda b, t: (b * nt + t, 0)),
                   pl.BlockSpec((1, CONV_K - 1, CONV_WIDTH), lambda b, t: (b, 0, 0))),
        scratch_shapes=[
            pltpu.VMEM((V7X_SUBLANES, CONV_HALO + CONV_TT + V7X_SUBLANES, CONV_WIDTH), F32),
            pltpu.VMEM((CONV_RC, CONV_WIDTH), F32),
            pltpu.VMEM((CONV_K, V7X_SUBLANES, CONV_WIDTH), F32),
        ],
        compiler_params=_params(2),
        name="conv_prompt",
    )(glu, conv_w, conv_b, ln_g, ln_b)


CONV_SB = 16


def _conv_sample_kernel(x_ref, st_ref, w_ref, cb_ref, g_ref, b_ref, o_ref, nst_ref):
    nb = CONV_K - 1
    x = x_ref[...]
    acc = x * w_ref[nb:CONV_K, :] + cb_ref[...]
    for k in range(nb):
        acc = acc + st_ref[k] * w_ref[k:k + 1, :]
    o_ref[...] = _ln_swish(acc, g_ref[...], b_ref[...]).astype(o_ref.dtype)
    for k in range(nb - 1):
        nst_ref[k] = st_ref[k + 1]
    nst_ref[nb - 1] = x


def conv_sample(glu, state_tsc, conv_w, conv_b, ln_g, ln_b):
    vec = pl.BlockSpec((1, CONV_WIDTH), lambda i: (0, 0))
    st_spec = pl.BlockSpec((CONV_K - 1, CONV_SB, CONV_WIDTH), lambda i: (0, i, 0))
    return pl.pallas_call(
        _conv_sample_kernel,
        out_shape=(jax.ShapeDtypeStruct((S_ROWS, CONV_WIDTH), BF16),
                   jax.ShapeDtypeStruct((CONV_K - 1, S_ROWS, CONV_WIDTH), F32)),
        grid=(S_ROWS // CONV_SB,),
        in_specs=[
            pl.BlockSpec((CONV_SB, CONV_WIDTH), lambda i: (i, 0)),
            st_spec,
            pl.BlockSpec((CONV_K, CONV_WIDTH), lambda i: (0, 0)),
            vec, vec, vec,
        ],
        out_specs=(pl.BlockSpec((CONV_SB, CONV_WIDTH), lambda i: (i, 0)), st_spec),
        compiler_params=_params(1),
        name="conv_sample",
    )(glu, state_tsc, conv_w, conv_b, ln_g, ln_b)


S5_GB = 16
S5_CH = S5_GB * SSM_GROUP
S5_ST = S5_GB * SSM_STATE
S5_NB = SSM_GROUPS // S5_GB
S5_SEG = V7X_SUBLANES
S5_SEGLEN = SEQ // S5_SEG
S5_TC = 64
S5_NC = S5_SEGLEN // S5_TC
S5_CR = S5_TC * S5_SEG
S5_LH = S5_CH // V7X_LANES


def _s5_discretize(lr_ref, li_ref, ldt_ref):
    lr, li = lr_ref[0], li_ref[0]
    dt = jnp.exp(ldt_ref[0])
    mag = jnp.exp(lr * dt)
    a_re, a_im = mag * jnp.cos(li * dt), mag * jnp.sin(li * dt)
    den = lr * lr + li * li
    nr, ni = a_re - 1.0, a_im
    f_re = (nr * lr + ni * li) / den
    f_im = (ni * lr - nr * li) / den
    return a_re, a_im, f_re, f_im


def _exact_transpose(x):
    k = x.shape[1]
    eye = jnp.where(lax.broadcasted_iota(jnp.int32, (k, k), 0) == lax.broadcasted_iota(jnp.int32, (k, k), 1),
                    1.0, 0.0).astype(BF16)

    def through_identity(piece):
        return lax.dot_general(eye, piece, (((1,), (1,)), ((), ())), preferred_element_type=F32)

    hi = x.astype(BF16)
    rest = x - hi.astype(F32)
    mid = rest.astype(BF16)
    lo = (rest - mid.astype(F32)).astype(BF16)
    return through_identity(hi) + through_identity(mid) + through_identity(lo)


def _s5_fill_weights(f_re, f_im, bre_ref, bim_ref, cre_ref, cim_ref, bb_ref, cc_ref):
    def over_groups(v):
        return jnp.concatenate([v] * S5_GB, axis=0)

    bt_re, bt_im = _exact_transpose(bre_ref[...]), _exact_transpose(bim_ref[...])
    same = (lax.broadcasted_iota(jnp.int32, (S5_CH, S5_ST), 0) // SSM_GROUP
            == lax.broadcasted_iota(jnp.int32, (S5_CH, S5_ST), 1) // SSM_STATE)
    bb_ref[:, 0:S5_ST] = jnp.where(same, over_groups(f_re * bt_re - f_im * bt_im), 0.0).astype(BF16)
    bb_ref[:, S5_ST:2 * S5_ST] = jnp.where(same, over_groups(f_re * bt_im + f_im * bt_re), 0.0).astype(BF16)

    ct_re, ct_im = _exact_transpose(cre_ref[...]), _exact_transpose(cim_ref[...])
    same = (lax.broadcasted_iota(jnp.int32, (S5_ST, S5_CH), 0) // SSM_STATE
            == lax.broadcasted_iota(jnp.int32, (S5_ST, S5_CH), 1) // SSM_GROUP)
    cc_ref[0:S5_ST, :] = jnp.where(same, over_groups(ct_re), 0.0).astype(BF16)
    cc_ref[S5_ST:2 * S5_ST, :] = jnp.where(same, over_groups(-ct_im), 0.0).astype(BF16)


def _s5_prompt_kernel(u_ref, lr_ref, li_ref, ldt_ref, bdre_ref, bdim_ref, cdre_ref, cdim_ref, d_ref,
                      sg_ref, sgb_ref, hre_ref, him_ref,
                      bb_ref, cc_ref, ul_ref, sgl_ref, lhs_ref, bu_ref, hch_ref):
    a_re, a_im, f_re, f_im = _s5_discretize(lr_ref, li_ref, ldt_ref)

    @pl.when(pl.program_id(1) == 0)
    def _():
        _s5_fill_weights(f_re, f_im, bdre_ref, bdim_ref, cdre_ref, cdim_ref, bb_ref, cc_ref)

    are8 = jnp.broadcast_to(a_re, (S5_SEG, S5_ST))
    aim8 = jnp.broadcast_to(a_im, (S5_SEG, S5_ST))
    d = d_ref[0]
    for hh in range(S5_LH):
        ul_ref[hh] = u_ref[:, hh * V7X_LANES:(hh + 1) * V7X_LANES]

    def chunk_rows(c):
        return slice(c * S5_CR, (c + 1) * S5_CR)

    def project_in(c):
        for tl in range(S5_TC):
            t = c * S5_TC + tl
            for hh in range(S5_LH):
                lhs_ref[t * S5_SEG:(t + 1) * S5_SEG, hh * V7X_LANES:(hh + 1) * V7X_LANES] = (
                    ul_ref[hh, pl.ds(t, S5_SEG, stride=S5_SEGLEN), :])
        bu_ref[chunk_rows(c), :] = _dot(lhs_ref[chunk_rows(c), :].astype(BF16), bb_ref[...])

    def scan(c, hr, hi, keep):
        for tl in range(S5_TC):
            t = c * S5_TC + tl
            rows = slice(t * S5_SEG, (t + 1) * S5_SEG)
            br = bu_ref[rows, 0:S5_ST]
            bi = bu_ref[rows, S5_ST:2 * S5_ST]
            hr, hi = are8 * hr - aim8 * hi + br, are8 * hi + aim8 * hr + bi
            if keep:
                crow = slice(tl * S5_SEG, (tl + 1) * S5_SEG)
                hch_ref[c % 2, crow, 0:S5_ST] = hr
                hch_ref[c % 2, crow, S5_ST:2 * S5_ST] = hi
        return hr, hi

    def project_out(c):
        y = _dot(hch_ref[c % 2].astype(BF16), cc_ref[...]) + d * lhs_ref[chunk_rows(c), :]
        lhs_ref[chunk_rows(c), :] = jax.nn.gelu(y)
        for tl in range(S5_TC):
            t = c * S5_TC + tl
            for hh in range(S5_LH):
                sgl_ref[hh, pl.ds(t, S5_SEG, stride=S5_SEGLEN), :] = (
                    lhs_ref[t * S5_SEG:(t + 1) * S5_SEG, hh * V7X_LANES:(hh + 1) * V7X_LANES])

    zeros = jnp.zeros((S5_SEG, S5_ST), F32)
    e_re, e_im = zeros, zeros
    project_in(0)
    for c in range(S5_NC):
        if c + 1 < S5_NC:
            project_in(c + 1)
        e_re, e_im = scan(c, e_re, e_im, keep=False)

    p_re, p_im = a_re, a_im
    for _ in range(int(math.log2(S5_SEGLEN))):
        p_re, p_im = p_re * p_re - p_im * p_im, 2.0 * p_re * p_im
    seg = lax.broadcasted_iota(jnp.int32, (S5_SEG, S5_ST), 0)
    qr = qi = jnp.zeros((1, S5_ST), F32)
    h_re = h_im = zeros
    for s in range(1, S5_SEG):
        qr, qi = (p_re * qr - p_im * qi + e_re[s - 1:s, :],
                  p_re * qi + p_im * qr + e_im[s - 1:s, :])
        h_re = jnp.where(seg == s, qr, h_re)
        h_im = jnp.where(seg == s, qi, h_im)

    for c in range(S5_NC):
        h_re, h_im = scan(c, h_re, h_im, keep=True)
        if c > 0:
            project_out(c - 1)
    project_out(S5_NC - 1)

    hre_ref[0] = h_re[S5_SEG - 1:S5_SEG, :]
    him_ref[0] = h_im[S5_SEG - 1:S5_SEG, :]
    for hh in range(S5_LH):
        sg_ref[:, hh * V7X_LANES:(hh + 1) * V7X_LANES] = sgl_ref[hh]
        sgb_ref[:, hh * V7X_LANES:(hh + 1) * V7X_LANES] = sgl_ref[hh].astype(BF16)


def _s5_param_specs(idx):
    def vec(width):
        return pl.BlockSpec((1, 1, width), lambda *g: (idx(*g), 0, 0))

    def rows(shape):
        return pl.BlockSpec(shape, lambda *g: (idx(*g), 0))

    return [vec(S5_ST), vec(S5_ST), vec(S5_ST),
            rows((S5_ST, SSM_GROUP)), rows((S5_ST, SSM_GROUP)),
            rows((S5_CH, SSM_STATE)), rows((S5_CH, SSM_STATE)), vec(S5_CH)]


def s5_prompt(u, s5p):
    return pl.pallas_call(
        _s5_prompt_kernel,
        out_shape=(jax.ShapeDtypeStruct((P_ROWS, SSM_WIDTH), F32),
                   jax.ShapeDtypeStruct((P_ROWS, SSM_WIDTH), BF16),
                   jax.ShapeDtypeStruct((BATCH, 1, SSM_GROUPS * SSM_STATE), F32),
                   jax.ShapeDtypeStruct((BATCH, 1, SSM_GROUPS * SSM_STATE), F32)),
        grid=(S5_NB, BATCH),
        in_specs=[pl.BlockSpec((SEQ, S5_CH), lambda j, b: (b, j))] + _s5_param_specs(lambda j, b: j),
        out_specs=(pl.BlockSpec((SEQ, S5_CH), lambda j, b: (b, j)),
                   pl.BlockSpec((SEQ, S5_CH), lambda j, b: (b, j)),
                   pl.BlockSpec((1, 1, S5_ST), lambda j, b: (b, 0, j)),
                   pl.BlockSpec((1, 1, S5_ST), lambda j, b: (b, 0, j))),
        scratch_shapes=[
            pltpu.VMEM((S5_CH, 2 * S5_ST), BF16),
            pltpu.VMEM((2 * S5_ST, S5_CH), BF16),
            pltpu.VMEM((S5_LH, SEQ, V7X_LANES), F32),
            pltpu.VMEM((S5_LH, SEQ, V7X_LANES), F32),
            pltpu.VMEM((SEQ, S5_CH), F32),
            pltpu.VMEM((SEQ, 2 * S5_ST), F32),
            pltpu.VMEM((2, S5_CR, 2 * S5_ST), F32),
        ],
        compiler_params=_params(2),
        name="s5_prompt",
    )(u, *s5p)


def _s5_sample_kernel(u_ref, h0re_ref, h0im_ref, lr_ref, li_ref, ldt_ref, bdre_ref, bdim_ref,
                      cdre_ref, cdim_ref, d_ref, sg_ref, sgb_ref, hre_ref, him_ref, bb_ref, cc_ref):
    a_re, a_im, f_re, f_im = _s5_discretize(lr_ref, li_ref, ldt_ref)
    _s5_fill_weights(f_re, f_im, bdre_ref, bdim_ref, cdre_ref, cdim_ref, bb_ref, cc_ref)
    u = u_ref[...]
    bu = _dot(u.astype(BF16), bb_ref[...])
    h0r, h0i = h0re_ref[...], h0im_ref[...]
    hr = a_re * h0r - a_im * h0i + bu[:, 0:S5_ST]
    hi = a_re * h0i + a_im * h0r + bu[:, S5_ST:2 * S5_ST]
    hre_ref[...] = hr
    him_ref[...] = hi
    y = _dot(hr.astype(BF16), cc_ref[0:S5_ST, :]) + _dot(hi.astype(BF16), cc_ref[S5_ST:2 * S5_ST, :])
    sg = jax.nn.gelu(y + d_ref[0] * u)
    sg_ref[...] = sg
    sgb_ref[...] = sg.astype(BF16)


def s5_sample(u, h0_re, h0_im, s5p):
    st = pl.BlockSpec((S_ROWS, S5_ST), lambda j: (0, j))
    return pl.pallas_call(
        _s5_sample_kernel,
        out_shape=(jax.ShapeDtypeStruct((S_ROWS, SSM_WIDTH), F32),
                   jax.ShapeDtypeStruct((S_ROWS, SSM_WIDTH), BF16),
                   jax.ShapeDtypeStruct((S_ROWS, SSM_GROUPS * SSM_STATE), F32),
                   jax.ShapeDtypeStruct((S_ROWS, SSM_GROUPS * SSM_STATE), F32)),
        grid=(S5_NB,),
        in_specs=[pl.BlockSpec((S_ROWS, S5_CH), lambda j: (0, j)), st, st]
        + _s5_param_specs(lambda j: j),
        out_specs=(pl.BlockSpec((S_ROWS, S5_CH), lambda j: (0, j)),
                   pl.BlockSpec((S_ROWS, S5_CH), lambda j: (0, j)), st, st),
        scratch_shapes=[
            pltpu.VMEM((S5_CH, 2 * S5_ST), BF16),
            pltpu.VMEM((2 * S5_ST, S5_CH), BF16),
        ],
        compiler_params=_params(1),
        name="s5_sample",
    )(u, h0_re, h0_im, *s5p)


def s5_block_params(lam_re, lam_im, log_dt, b_re, b_im, c_re, c_im, d_skip):
    n_state = SSM_GROUPS * SSM_STATE
    return (lam_re.reshape(S5_NB, 1, S5_ST), lam_im.reshape(S5_NB, 1, S5_ST),
            jnp.repeat(log_dt, SSM_STATE).reshape(S5_NB, 1, S5_ST),
            b_re.reshape(n_state, SSM_GROUP), b_im.reshape(n_state, SSM_GROUP),
            c_re.reshape(SSM_WIDTH, SSM_STATE), c_im.reshape(SSM_WIDTH, SSM_STATE),
            d_skip.reshape(S5_NB, 1, S5_CH))


SSM_GLU_TN = 512


def _ssm_glu_kernel(ap_hbm, as_ref, w_ref, tp_ref, ts_ref, op_ref, os_ref, wc_ref, r_ref, ap_ref, sem):
    tile = _RowTile([ap_hbm], [ap_ref], [sem], SSM_WIDTH // SSM_GLU_TN)

    def body(col):
        wc_ref[...] = w_ref[...].astype(BF16)

        def mm(k):
            r_ref[_sub_rows(k), :] = _dot(ap_ref[_sub_rows(k), :], wc_ref[...])

        def epi(k):
            op_ref[_sub_rows(k), :] = (
                tp_ref[_sub_rows(k), :] * jax.nn.sigmoid(r_ref[_sub_rows(k), :])).astype(op_ref.dtype)

        _interleave(NSUB, tile.mm_hooks(col, mm), epi)

        @_on_last_row_tile
        def _():
            z = _dot(as_ref[...], wc_ref[...])
            os_ref[...] = (ts_ref[...] * jax.nn.sigmoid(z)).astype(os_ref.dtype)

    tile.run(body)


def ssm_glu(sg_p, sg_s, sgb_p, sgb_s, w_glu):
    tn = SSM_GLU_TN
    return pl.pallas_call(
        _ssm_glu_kernel,
        out_shape=(jax.ShapeDtypeStruct((P_ROWS, SSM_WIDTH), BF16),
                   jax.ShapeDtypeStruct((S_ROWS, SSM_WIDTH), BF16)),
        grid=(NPT, SSM_WIDTH // tn),
        in_specs=[
            _ANY,
            _resident((S_ROWS, SSM_WIDTH), lambda i, j: (0, 0)),
            pl.BlockSpec((SSM_WIDTH, tn), lambda i, j: (0, j)),
            pl.BlockSpec((TM, tn), lambda i, j: (i, j)),
            pl.BlockSpec((S_ROWS, tn), lambda i, j: (0, j)),
        ],
        out_specs=(pl.BlockSpec((TM, tn), lambda i, j: (i, j)),
                   pl.BlockSpec((S_ROWS, tn), _sample_cols)),
        scratch_shapes=[pltpu.VMEM((SSM_WIDTH, tn), BF16), pltpu.VMEM((TM, tn), F32)]
        + _row_tile_scratch([SSM_WIDTH]),
        compiler_params=_params(2),
        name="ssm_glu",
    )(sgb_p, sgb_s, w_glu, sg_p, sg_s)


MERGE_TN = 256


def _merge_kernel(cp_hbm, cs_ref, yp_hbm, ys_ref, wa_ref, wb_ref, gap_ref, gbp_ref, gas_ref, gbs_ref,
                  op_ref, os_ref, wc_ref, r_ref, cp_ref, yp_ref, csem, ysem):
    tile = _RowTile([cp_hbm, yp_hbm], [cp_ref, yp_ref], [csem, ysem], D_MODEL // MERGE_TN)

    def body(col):
        wc_ref[0] = wa_ref[...].astype(BF16)
        wc_ref[1] = wb_ref[...].astype(BF16)

        def mm(k):
            r_ref[0, _sub_rows(k), :] = _dot(cp_ref[_sub_rows(k), :], wc_ref[0])
            r_ref[1, _sub_rows(k), :] = _dot(yp_ref[_sub_rows(k), :], wc_ref[1])

        def epi(k):
            rows = _sub_rows(k)
            op_ref[rows, :] = (gap_ref[rows, :] * r_ref[0, rows, :]
                               + gbp_ref[rows, :] * r_ref[1, rows, :]).astype(op_ref.dtype)

        _interleave(NSUB, tile.mm_hooks(col, mm), epi)

        @_on_last_row_tile
        def _():
            ya = _dot(cs_ref[...], wc_ref[0])
            yb = _dot(ys_ref[...], wc_ref[1])
            os_ref[...] = (gas_ref[...] * ya + gbs_ref[...] * yb).astype(os_ref.dtype)

    tile.run(body)


def merge(c_p, c_s, yg_p, yg_s, w_conv_out, w_ssm_out, gates_p, gates_s):
    tn = MERGE_TN
    nj = D_MODEL // tn
    return pl.pallas_call(
        _merge_kernel,
        out_shape=(jax.ShapeDtypeStruct((P_ROWS, D_MODEL), BF16),
                   jax.ShapeDtypeStruct((S_ROWS, D_MODEL), BF16)),
        grid=(NPT, nj),
        in_specs=[
            _ANY,
            _resident((S_ROWS, CONV_WIDTH), lambda i, j: (0, 0)),
            _ANY,
            _resident((S_ROWS, SSM_WIDTH), lambda i, j: (0, 0)),
            pl.BlockSpec((CONV_WIDTH, tn), lambda i, j: (0, j)),
            pl.BlockSpec((SSM_WIDTH, tn), lambda i, j: (0, j)),
            pl.BlockSpec((TM, tn), lambda i, j: (i, j)),
            pl.BlockSpec((TM, tn), lambda i, j: (i, j + nj)),
            pl.BlockSpec((S_ROWS, tn), lambda i, j: (0, j)),
            pl.BlockSpec((S_ROWS, tn), lambda i, j: (0, j + nj)),
        ],
        out_specs=(pl.BlockSpec((TM, tn), lambda i, j: (i, j)),
                   pl.BlockSpec((S_ROWS, tn), _sample_cols)),
        scratch_shapes=[pltpu.VMEM((2, CONV_WIDTH, tn), BF16), pltpu.VMEM((2, TM, tn), F32)]
        + _row_tile_scratch([CONV_WIDTH, SSM_WIDTH]),
        compiler_params=_params(2),
        name="merge",
    )(c_p, c_s, yg_p, yg_s, w_conv_out, w_ssm_out, gates_p, gates_p, gates_s, gates_s)


OPROJ_TN = 512


def _oproj_kernel(ap_hbm, as_ref, w_ref, xp_ref, xs_ref, op_ref, os_ref, wc_ref, ap_ref, sem):
    tile = _RowTile([ap_hbm], [ap_ref], [sem], D_MODEL // OPROJ_TN)

    def body(col):
        wc_ref[...] = w_ref[...].astype(BF16)

        def mm(k):
            op_ref[_sub_rows(k), :] = _dot(ap_ref[_sub_rows(k), :], wc_ref[...])

        def epi(k):
            op_ref[_sub_rows(k), :] = op_ref[_sub_rows(k), :] + xp_ref[_sub_rows(k), :]

        _interleave(NSUB, tile.mm_hooks(col, mm), epi)

        @_on_last_row_tile
        def _():
            os_ref[...] = xs_ref[...] + _dot(as_ref[...], wc_ref[...])

    tile.run(body)


def oproj(merged_p, merged_s, w_o, xp, xs):
    tn = OPROJ_TN
    return pl.pallas_call(
        _oproj_kernel,
        out_shape=(jax.ShapeDtypeStruct((P_ROWS, D_MODEL), F32),
                   jax.ShapeDtypeStruct((S_ROWS, D_MODEL), F32)),
        grid=(NPT, D_MODEL // tn),
        in_specs=[
            _ANY,
            _resident((S_ROWS, D_MODEL), lambda i, j: (0, 0)),
            pl.BlockSpec((D_MODEL, tn), lambda i, j: (0, j)),
            pl.BlockSpec((TM, tn), lambda i, j: (i, j)),
            pl.BlockSpec((S_ROWS, tn), lambda i, j: (0, j)),
        ],
        out_specs=(pl.BlockSpec((TM, tn), lambda i, j: (i, j)),
                   pl.BlockSpec((S_ROWS, tn), _sample_cols)),
        scratch_shapes=[pltpu.VMEM((D_MODEL, tn), BF16)] + _row_tile_scratch([D_MODEL]),
        compiler_params=_params(2),
        name="oproj",
    )(merged_p, merged_s, w_o, xp, xs)


DOWN_TM = 512
DOWN_NPT = P_ROWS // DOWN_TM


def _ffn_down_kernel(ap_ref, as_ref, w_ref, xp_ref, xs_ref, op_ref, os_ref):
    op_ref[...] = xp_ref[...] + _dot(ap_ref[...], w_ref[...])

    @pl.when(pl.program_id(0) == DOWN_NPT - 1)
    def _():
        os_ref[...] = xs_ref[...] + _dot(as_ref[...], w_ref[...])


def ffn_down(h_p, h_s, w_down_bf16, x1_p, x1_s):
    tn = 512
    last = DOWN_NPT - 1
    return pl.pallas_call(
        _ffn_down_kernel,
        out_shape=(jax.ShapeDtypeStruct((P_ROWS, D_MODEL), F32),
                   jax.ShapeDtypeStruct((S_ROWS, D_MODEL), F32)),
        grid=(DOWN_NPT, D_MODEL // tn),
        in_specs=[
            pl.BlockSpec((DOWN_TM, D_FF), lambda i, j: (i, 0)),
            _resident((S_ROWS, D_FF), lambda i, j: (0, 0)),
            pl.BlockSpec((D_FF, tn), lambda i, j: (0, j)),
            pl.BlockSpec((DOWN_TM, tn), lambda i, j: (i, j)),
            pl.BlockSpec((S_ROWS, tn), lambda i, j: (0, j)),
        ],
        out_specs=(pl.BlockSpec((DOWN_TM, tn), lambda i, j: (i, j)),
                   pl.BlockSpec((S_ROWS, tn), lambda i, j: (0, jnp.where(i == last, j, 0)))),
        compiler_params=_params(2),
        name="ffn_down",
    )(h_p, h_s, w_down_bf16, x1_p, x1_s)


FFN_TN = 256
FFN_NJ = D_FF // FFN_TN
FFN_PAD = V7X_SUBLANES


def _ffn_up_kernel(ap_hbm, as_ref, wg_ref, wv_ref, s0_ref, s1_ref, cw_ref, cb_ref, wd_ref,
                   hp_ref, hs_ref, tail_ref, gs_ref, wdb_ref, wc_ref, g_ref, v_ref, ap_ref, sem):
    tn = FFN_TN
    tile = _RowTile([ap_hbm], [ap_ref], [sem], FFN_NJ)

    def body(col):
        wc_ref[:, 0:tn] = wg_ref[...].astype(BF16)
        wc_ref[:, tn:2 * tn] = wv_ref[...].astype(BF16)
        wdb_ref[...] = wd_ref[...].astype(BF16)
        cols = pl.ds(pl.multiple_of(pl.program_id(1) * tn, tn), tn)
        w0, w1, w2 = cw_ref[0:1, cols], cw_ref[1:2, cols], cw_ref[2:3, cols]
        cb = cb_ref[:, cols]
        g_ref[0:FFN_PAD, :] = jnp.zeros((FFN_PAD, tn), F32)

        def mm(k):
            r = _dot(ap_ref[_sub_rows(k), :], wc_ref[...])
            g_ref[_sub_rows(k, FFN_PAD), :] = r[:, 0:tn]
            v_ref[_sub_rows(k), :] = r[:, tn:2 * tn]

        def epi(k):
            gc = (w0 * g_ref[_sub_rows(k, FFN_PAD - 2), :] + w1 * g_ref[_sub_rows(k, FFN_PAD - 1), :]
                  + w2 * g_ref[_sub_rows(k, FFN_PAD), :] + cb)
            hp_ref[_sub_rows(k), :] = (gc * jax.nn.sigmoid(gc) * v_ref[_sub_rows(k), :]).astype(hp_ref.dtype)

        _interleave(NSUB, tile.mm_hooks(col, mm), epi)
        tail_ref[0, :, cols] = g_ref[TM:TM + FFN_PAD, :]

        @_on_last_row_tile
        def _():
            r = _dot(as_ref[...], wc_ref[...])
            gate, val = r[:, 0:tn], r[:, tn:2 * tn]
            gc = w0 * s0_ref[...] + w1 * s1_ref[...] + w2 * gate + cb
            hs_ref[...] = (gc * jax.nn.sigmoid(gc) * val).astype(hs_ref.dtype)
            gs_ref[...] = gate

    tile.run(body)


def ffn_up(xn2_p, xn2_s, w_up, ffn_old, ffn_conv_w, ffn_conv_b, w_down):
    nj = FFN_NJ
    tn = FFN_TN
    wd_rows = D_FF // (NPT * nj)

    def wd_slab(i, j):
        return (i * nj + j, 0)

    whole_row = lambda i, j: (0, 0)

    return pl.pallas_call(
        _ffn_up_kernel,
        out_shape=(jax.ShapeDtypeStruct((P_ROWS, D_FF), BF16),
                   jax.ShapeDtypeStruct((S_ROWS, D_FF), BF16),
                   jax.ShapeDtypeStruct((NPT, FFN_PAD, D_FF), F32),
                   jax.ShapeDtypeStruct((S_ROWS, D_FF), F32),
                   jax.ShapeDtypeStruct((D_FF, D_MODEL), BF16)),
        grid=(NPT, nj),
        in_specs=[
            _ANY,
            _resident((S_ROWS, D_MODEL), lambda i, j: (0, 0)),
            pl.BlockSpec((D_MODEL, tn), lambda i, j: (0, j)),
            pl.BlockSpec((D_MODEL, tn), lambda i, j: (0, j + nj)),
            pl.BlockSpec((S_ROWS, tn), _sample_cols),
            pl.BlockSpec((S_ROWS, tn), _sample_cols),
            pl.BlockSpec((FFN_K, D_FF), whole_row),
            pl.BlockSpec((1, D_FF), whole_row),
            pl.BlockSpec((wd_rows, D_MODEL), wd_slab),
        ],
        out_specs=(
            pl.BlockSpec((TM, tn), lambda i, j: (i, j)),
            pl.BlockSpec((S_ROWS, tn), _sample_cols),
            pl.BlockSpec((1, FFN_PAD, D_FF), lambda i, j: (i, 0, 0)),
            pl.BlockSpec((S_ROWS, tn), _sample_cols),
            pl.BlockSpec((wd_rows, D_MODEL), wd_slab),
        ),
        scratch_shapes=[
            pltpu.VMEM((D_MODEL, 2 * tn), BF16),
            pltpu.VMEM((FFN_PAD + TM, tn), F32),
            pltpu.VMEM((TM, tn), F32),
        ] + _row_tile_scratch([D_MODEL]),
        compiler_params=_params(2),
        name="ffn_up",
    )(xn2_p, xn2_s, w_up, w_up, ffn_old[0], ffn_old[1], ffn_conv_w, ffn_conv_b, w_down)


def kernel(x_prompt, x_sample, state_conv, state_ssm_re, state_ssm_im, state_ffn_conv,
           norm_mix_g, w_in, conv_w, conv_b, ln_g, ln_b, w_conv_out,
           lam_re, lam_im, log_dt, b_re, b_im, c_re, c_im, d_skip, w_glu, w_ssm_out, w_o,
           norm_ffn_g, w_up, ffn_conv_w, ffn_conv_b, w_down, final_norm_g):
    xp = x_prompt.reshape(P_ROWS, D_MODEL)
    xs = x_sample.reshape(S_ROWS, D_MODEL)

    def row(v):
        return v.reshape(1, -1)

    xn_p, xn_s = rownorm_pair(xp, xs, row(norm_mix_g[0]), BF16)
    glu_p, glu_s = inproj_glu(xn_p, xn_s, w_in[0])
    u_p, u_s = inproj_cols(xn_p, xn_s, w_in[0], 2 * CONV_WIDTH, SSM_WIDTH, False, "inproj_ssm")
    gates_p, gates_s = inproj_cols(xn_p, xn_s, w_in[0], 2 * CONV_WIDTH + SSM_WIDTH, 2 * D_MODEL, True,
                                   "inproj_gates")

    conv_vecs = (conv_w[0], row(conv_b[0]), row(ln_g[0]), row(ln_b[0]))
    c_p, conv_p = conv_prompt(glu_p, *conv_vecs)
    c_s, conv_s = conv_sample(glu_s, state_conv[0].transpose(1, 0, 2), *conv_vecs)
    conv_s = conv_s.transpose(1, 0, 2)

    s5p = s5_block_params(lam_re[0], lam_im[0], log_dt[0], b_re[0], b_im[0], c_re[0], c_im[0], d_skip[0])
    n_state = SSM_GROUPS * SSM_STATE
    sg_p, sgb_p, ssr_p, ssi_p = s5_prompt(u_p, s5p)
    sg_s, sgb_s, ssr_s, ssi_s = s5_sample(u_s, state_ssm_re[0].reshape(S_ROWS, n_state),
                                          state_ssm_im[0].reshape(S_ROWS, n_state), s5p)
    yg_p, yg_s = ssm_glu(sg_p, sg_s, sgb_p, sgb_s, w_glu[0])

    merged_p, merged_s = merge(c_p, c_s, yg_p, yg_s, w_conv_out[0], w_ssm_out[0], gates_p, gates_s)
    x1_p, x1_s = oproj(merged_p, merged_s, w_o[0], xp, xs)

    xn2_p, xn2_s = rownorm_pair(x1_p, x1_s, row(norm_ffn_g[0]), BF16)
    ffn_old = [state_ffn_conv[0, :, k, :] for k in range(FFN_K - 1)]
    h_p, h_s, gate_tail, gate_s, w_down_bf16 = ffn_up(
        xn2_p, xn2_s, w_up[0], ffn_old, ffn_conv_w[0], row(ffn_conv_b[0]), w_down[0])
    x2_p, x2_s = ffn_down(h_p, h_s, w_down_bf16, x1_p, x1_s)
    y_p, y_s = rownorm_pair(x2_p, x2_s, row(final_norm_g), F32)

    ffn_p = gate_tail[:, FFN_PAD - (FFN_K - 1):, :]
    ffn_s = jnp.stack([ffn_old[1], gate_s], axis=1)
    state_shape = (1, -1, SSM_GROUPS, SSM_STATE)
    return (y_p.reshape(BATCH, SEQ, D_MODEL), y_s.reshape(DEC_BATCH, 1, D_MODEL),
            conv_p[None], conv_s[None],
            ssr_p.reshape(state_shape), ssi_p.reshape(state_shape),
            ssr_s.reshape(state_shape), ssi_s.reshape(state_shape),
            ffn_p[None], ffn_s[None])
```

```python
import functools
import math

import jax
import jax.numpy as jnp
from jax import lax
from jax.experimental import pallas as pl
from jax.experimental.pallas import tpu as pltpu

D_MODEL = 4096
BATCH = 4
SEQ = 2048
DEC_BATCH = 128
CONV_WIDTH = D_MODEL // 2
CONV_K = 31
SSM_WIDTH = D_MODEL // 2
SSM_GROUP = 16
SSM_GROUPS = SSM_WIDTH // SSM_GROUP
SSM_STATE = 64
D_FF = 11008
FFN_K = 3
EPS = 1e-6

P_ROWS = BATCH * SEQ
S_ROWS = DEC_BATCH

V7X_SUBLANES = 8
V7X_LANES = 128
V7X_VMEM_LIMIT = 58 * 1024 * 1024

TM = SEQ
NPT = P_ROWS // TM
SUB_SIZES = (768, 768, 384, 128)
SUB_STARTS = tuple(sum(SUB_SIZES[:k]) for k in range(len(SUB_SIZES)))
NSUB = len(SUB_SIZES)
assert sum(SUB_SIZES) == TM

BF16 = jnp.bfloat16
F32 = jnp.float32


def _params(n_grid_dims):
    return pltpu.CompilerParams(
        dimension_semantics=("arbitrary",) * n_grid_dims,
        vmem_limit_bytes=V7X_VMEM_LIMIT)


def _dot(a, b):
    return jnp.dot(a, b, preferred_element_type=F32)


def _rms(x, g):
    return x * lax.rsqrt(jnp.mean(x * x, axis=-1, keepdims=True) + EPS) * g


def _interleave(n, mm, epi):
    mm(0)
    for k in range(1, n):
        mm(k)
        epi(k - 1)
    epi(n - 1)


def _sub_rows(k, offset=0):
    return slice(offset + SUB_STARTS[k], offset + SUB_STARTS[k] + SUB_SIZES[k])


class _RowTile:
    def __init__(self, hbm_refs, bufs, sems, ncol, sizes=SUB_SIZES):
        self.hbm_refs, self.bufs, self.sems, self.ncol = hbm_refs, bufs, sems, ncol
        self.sizes = sizes
        self.starts = tuple(sum(sizes[:k]) for k in range(len(sizes)))
        self.tm = sum(sizes)
        self.i, self.j = pl.program_id(0), pl.program_id(1)

    def rows(self, k):
        return slice(self.starts[k], self.starts[k] + self.sizes[k])

    def _copies(self, tile, k):
        return [pltpu.make_async_copy(hbm.at[pl.ds(tile * self.tm + self.starts[k], self.sizes[k]), :],
                                      buf.at[pl.ds(self.starts[k], self.sizes[k]), :], sem.at[k])
                for hbm, buf, sem in zip(self.hbm_refs, self.bufs, self.sems)]

    def fetch_first_tile(self):
        @pl.when((self.i == 0) & (self.j == 0))
        def _():
            for k in range(len(self.sizes)):
                for c in self._copies(0, k):
                    c.start()

    def wait(self, k):
        for c in self._copies(self.i, k):
            c.wait()

    def fetch_next(self, k):
        @pl.when(self.i < P_ROWS // self.tm - 1)
        def _():
            for c in self._copies(self.i + 1, k):
                c.start()

    def run(self, body):
        assert self.ncol >= 2
        self.fetch_first_tile()
        pl.when(self.j == 0)(functools.partial(body, "first"))
        pl.when((self.j > 0) & (self.j < self.ncol - 1))(functools.partial(body, "mid"))
        pl.when(self.j == self.ncol - 1)(functools.partial(body, "last"))

    def mm_hooks(self, col, mm):
        def wrapped(k):
            if col == "first":
                self.wait(k)
            mm(k)
            if col == "last":
                self.fetch_next(k)
        return wrapped


_ANY = pl.BlockSpec(memory_space=pl.ANY)


def _row_tile_scratch(widths, sizes=SUB_SIZES):
    return ([pltpu.VMEM((sum(sizes), w), BF16) for w in widths]
            + [pltpu.SemaphoreType.DMA((len(sizes),)) for _ in widths])


def _resident(shape, index_map):
    return pl.BlockSpec(shape, index_map, pipeline_mode=pl.Buffered(1))


def _sample_cols(i, j):
    return (0, jnp.where(i == NPT - 1, j, 0))


def _on_last_row_tile(fn):
    pl.when(pl.program_id(0) == NPT - 1)(fn)


def _rownorm_kernel(x_ref, g_ref, o_ref):
    o_ref[...] = _rms(x_ref[...], g_ref[...]).astype(o_ref.dtype)


def rownorm(x, g, out_dtype, row_block):
    rows = x.shape[0]
    return pl.pallas_call(
        _rownorm_kernel,
        out_shape=jax.ShapeDtypeStruct((rows, D_MODEL), out_dtype),
        grid=(rows // row_block,),
        in_specs=[
            pl.BlockSpec((row_block, D_MODEL), lambda i: (i, 0)),
            pl.BlockSpec((1, D_MODEL), lambda i: (0, 0)),
        ],
        out_specs=pl.BlockSpec((row_block, D_MODEL), lambda i: (i, 0)),
        compiler_params=_params(1),
        name="rownorm",
    )(x, g)


def rownorm_pair(xp, xs, g, out_dtype):
    return rownorm(xp, g, out_dtype, 512), rownorm(xs, g, out_dtype, S_ROWS)


GLU_TN = 256


def _inproj_glu_kernel(ap_hbm, as_ref, wa_ref, wg_ref, op_ref, os_ref, wc_ref, r_ref, ap_ref, sem):
    tn = GLU_TN
    tile = _RowTile([ap_hbm], [ap_ref], [sem], CONV_WIDTH // tn)

    def body(col):
        wc_ref[:, 0:tn] = wa_ref[...].astype(BF16)
        wc_ref[:, tn:2 * tn] = wg_ref[...].astype(BF16)

        def mm(k):
            r_ref[_sub_rows(k), :] = _dot(ap_ref[_sub_rows(k), :], wc_ref[...])

        def epi(k):
            r = r_ref[_sub_rows(k), :]
            op_ref[_sub_rows(k), :] = r[:, 0:tn] * jax.nn.sigmoid(r[:, tn:2 * tn])

        _interleave(NSUB, tile.mm_hooks(col, mm), epi)

        @_on_last_row_tile
        def _():
            r = _dot(as_ref[...], wc_ref[...])
            os_ref[...] = r[:, 0:tn] * jax.nn.sigmoid(r[:, tn:2 * tn])

    tile.run(body)


def inproj_glu(xn_p, xn_s, w_in):
    tn = GLU_TN
    nj = CONV_WIDTH // tn
    return pl.pallas_call(
        _inproj_glu_kernel,
        out_shape=(jax.ShapeDtypeStruct((P_ROWS, CONV_WIDTH), F32),
                   jax.ShapeDtypeStruct((S_ROWS, CONV_WIDTH), F32)),
        grid=(NPT, nj),
        in_specs=[
            _ANY,
            _resident((S_ROWS, D_MODEL), lambda i, j: (0, 0)),
            pl.BlockSpec((D_MODEL, tn), lambda i, j: (0, j)),
            pl.BlockSpec((D_MODEL, tn), lambda i, j: (0, j + nj)),
        ],
        out_specs=(pl.BlockSpec((TM, tn), lambda i, j: (i, j)),
                   pl.BlockSpec((S_ROWS, tn), _sample_cols)),
        scratch_shapes=[pltpu.VMEM((D_MODEL, 2 * tn), BF16), pltpu.VMEM((TM, 2 * tn), F32)]
        + _row_tile_scratch([D_MODEL]),
        compiler_params=_params(2),
        name="inproj_glu",
    )(xn_p, xn_s, w_in, w_in)


def _inproj_cols_kernel(ap_hbm, as_ref, w_ref, op_ref, os_ref, wc_ref, ap_ref, sem, *, squash, ncol):
    tile = _RowTile([ap_hbm], [ap_ref], [sem], ncol)

    def body(col):
        wc_ref[...] = w_ref[...].astype(BF16)

        def mm(k):
            op_ref[_sub_rows(k), :] = _dot(ap_ref[_sub_rows(k), :], wc_ref[...])

        def epi(k):
            if squash:
                op_ref[_sub_rows(k), :] = jax.nn.sigmoid(op_ref[_sub_rows(k), :])

        _interleave(NSUB, tile.mm_hooks(col, mm), epi)

        @_on_last_row_tile
        def _():
            r = _dot(as_ref[...], wc_ref[...])
            os_ref[...] = jax.nn.sigmoid(r) if squash else r

    tile.run(body)


def inproj_cols(xn_p, xn_s, w_in, col0, width, squash, name):
    tn = 512
    return pl.pallas_call(
        functools.partial(_inproj_cols_kernel, squash=squash, ncol=width // tn),
        out_shape=(jax.ShapeDtypeStruct((P_ROWS, width), F32),
                   jax.ShapeDtypeStruct((S_ROWS, width), F32)),
        grid=(NPT, width // tn),
        in_specs=[
            _ANY,
            _resident((S_ROWS, D_MODEL), lambda i, j: (0, 0)),
            pl.BlockSpec((D_MODEL, tn), lambda i, j: (0, j + col0 // tn)),
        ],
        out_specs=(pl.BlockSpec((TM, tn), lambda i, j: (i, j)),
                   pl.BlockSpec((S_ROWS, tn), _sample_cols)),
        scratch_shapes=[pltpu.VMEM((D_MODEL, tn), BF16)] + _row_tile_scratch([D_MODEL]),
        compiler_params=_params(2),
        name=name,
    )(xn_p, xn_s, w_in)


CONV_TT = 256
CONV_HALO = 32
CONV_RC = 32
CONV_LC = 256
CONV_PIECE = 32


def _ln_swish(c, g, b):
    mu = jnp.mean(c, axis=-1, keepdims=True)
    d = c - mu
    var = jnp.mean(d * d, axis=-1, keepdims=True)
    r = d * lax.rsqrt(var + EPS) * g + b
    return r * jax.nn.sigmoid(r)


def _conv_prompt_kernel(x_ref, w_ref, cb_ref, g_ref, b_ref, o_ref, st_ref, xs_ref, cbuf_ref, w8_ref):
    t = pl.program_id(1)
    nt = pl.num_programs(1)
    sub = V7X_SUBLANES

    @pl.when(t == 0)
    def _():
        xs_ref[0, 0:CONV_HALO, :] = jnp.zeros((CONV_HALO, CONV_WIDTH), F32)
        xs_ref[0, CONV_HALO + CONV_TT:CONV_HALO + CONV_TT + sub, :] = jnp.zeros((sub, CONV_WIDTH), F32)
        for k in range(CONV_K):
            w8_ref[k] = jnp.broadcast_to(w_ref[k:k + 1, :], (sub, CONV_WIDTH))

    xs_ref[0, CONV_HALO:CONV_HALO + CONV_TT, :] = x_ref[...]

    def shift_piece(p, carry):
        r = pl.multiple_of(p * CONV_PIECE, CONV_PIECE)
        piece = xs_ref[0, pl.ds(r, CONV_PIECE + sub), :]
        for m in range(1, sub):
            rolled = pltpu.roll(piece, CONV_PIECE + sub - m, axis=0)
            xs_ref[m, pl.ds(r, CONV_PIECE), :] = rolled[0:CONV_PIECE]
        return carry

    lax.fori_loop(0, (CONV_HALO + CONV_TT) // CONV_PIECE, shift_piece, 0)

    off = CONV_HALO - (CONV_K - 1)

    lane_chunks = [slice(l0, l0 + CONV_LC) for l0 in range(0, CONV_WIDTH, CONV_LC)]

    def chunk(c, carry):
        r0 = pl.multiple_of(c * CONV_RC, CONV_RC)
        part = jnp.zeros((CONV_RC, CONV_LC), F32)
        for lanes in lane_chunks:
            accs = [jnp.zeros((sub, CONV_LC), F32) for _ in range(CONV_RC // sub)]
            for k in range(CONV_K):
                o = off + k
                wk = w8_ref[k, :, lanes]
                for rg in range(CONV_RC // sub):
                    win = xs_ref[o % sub, pl.ds(r0 + (o // sub + rg) * sub, sub), lanes]
                    accs[rg] = accs[rg] + win * wk
            acc = jnp.concatenate(accs, axis=0) + cb_ref[:, lanes]
            cbuf_ref[:, lanes] = acc
            part = part + acc
        mu = jnp.sum(part, axis=-1, keepdims=True) / CONV_WIDTH
        part = jnp.zeros((CONV_RC, CONV_LC), F32)
        for lanes in lane_chunks:
            d = cbuf_ref[:, lanes] - mu
            part = part + d * d
        rstd = lax.rsqrt(jnp.sum(part, axis=-1, keepdims=True) / CONV_WIDTH + EPS)
        for lanes in lane_chunks:
            r = (cbuf_ref[:, lanes] - mu) * rstd * g_ref[:, lanes] + b_ref[:, lanes]
            o_ref[pl.ds(r0, CONV_RC), lanes] = (r * jax.nn.sigmoid(r)).astype(o_ref.dtype)
        return carry

    lax.fori_loop(0, CONV_TT // CONV_RC, chunk, 0)

    @pl.when(t == nt - 1)
    def _():
        st_ref[0] = xs_ref[off, CONV_TT:CONV_TT + CONV_K - 1, :]

    xs_ref[0, 0:CONV_HALO, :] = xs_ref[0, CONV_TT:CONV_TT + CONV_HALO, :]


def conv_prompt(glu, conv_w, conv_b, ln_g, ln_b):
    nt = SEQ // CONV_TT
    vec = pl.BlockSpec((1, CONV_WIDTH), lambda b, t: (0, 0))
    return pl.pallas_call(
        _conv_prompt_kernel,
        out_shape=(jax.ShapeDtypeStruct((P_ROWS, CONV_WIDTH), BF16),
                   jax.ShapeDtypeStruct((BATCH, CONV_K - 1, CONV_WIDTH), F32)),
        grid=(BATCH, nt),
        in_specs=[
            pl.BlockSpec((CONV_TT, CONV_WIDTH), lambda b, t: (b * nt + t, 0)),
            pl.BlockSpec((CONV_K, CONV_WIDTH), lambda b, t: (0, 0)),
            vec, vec, vec,
        ],
        out_specs=(pl.BlockSpec((CONV_TT, CONV_WIDTH), lambda b, t: (b * nt + t, 0)),
                   pl.BlockSpec((1, CONV_K - 1, CONV_WIDTH), lambda b, t: (b, 0, 0))),
        scratch_shapes=[
            pltpu.VMEM((V7X_SUBLANES, CONV_HALO + CONV_TT + V7X_SUBLANES, CONV_WIDTH), F32),
            pltpu.VMEM((CONV_RC, CONV_WIDTH), F32),
            pltpu.VMEM((CONV_K, V7X_SUBLANES, CONV_WIDTH), F32),
        ],
        compiler_params=_params(2),
        name="conv_prompt",
    )(glu, conv_w, conv_b, ln_g, ln_b)


CONV_SB = 16


def _conv_sample_kernel(x_ref, st_ref, w_ref, cb_ref, g_ref, b_ref, o_ref, nst_ref):
    nb = CONV_K - 1
    x = x_ref[...]
    acc = x * w_ref[nb:CONV_K, :] + cb_ref[...]
    for k in range(nb):
        acc = acc + st_ref[k] * w_ref[k:k + 1, :]
    o_ref[...] = _ln_swish(acc, g_ref[...], b_ref[...]).astype(o_ref.dtype)
    for k in range(nb - 1):
        nst_ref[k] = st_ref[k + 1]
    nst_ref[nb - 1] = x


def conv_sample(glu, state_tsc, conv_w, conv_b, ln_g, ln_b):
    vec = pl.BlockSpec((1, CONV_WIDTH), lambda i: (0, 0))
    st_spec = pl.BlockSpec((CONV_K - 1, CONV_SB, CONV_WIDTH), lambda i: (0, i, 0))
    return pl.pallas_call(
        _conv_sample_kernel,
        out_shape=(jax.ShapeDtypeStruct((S_ROWS, CONV_WIDTH), BF16),
                   jax.ShapeDtypeStruct((CONV_K - 1, S_ROWS, CONV_WIDTH), F32)),
        grid=(S_ROWS // CONV_SB,),
        in_specs=[
            pl.BlockSpec((CONV_SB, CONV_WIDTH), lambda i: (i, 0)),
            st_spec,
            pl.BlockSpec((CONV_K, CONV_WIDTH), lambda i: (0, 0)),
            vec, vec, vec,
        ],
        out_specs=(pl.BlockSpec((CONV_SB, CONV_WIDTH), lambda i: (i, 0)), st_spec),
        compiler_params=_params(1),
        name="conv_sample",
    )(glu, state_tsc, conv_w, conv_b, ln_g, ln_b)


S5_GB = 16
S5_CH = S5_GB * SSM_GROUP
S5_ST = S5_GB * SSM_STATE
S5_NB = SSM_GROUPS // S5_GB
S5_SEG = V7X_SUBLANES
S5_SEGLEN = SEQ // S5_SEG
S5_TC = 64
S5_NC = S5_SEGLEN // S5_TC
S5_CR = S5_TC * S5_SEG
S5_LH = S5_CH // V7X_LANES


def _s5_discretize(lr_ref, li_ref, ldt_ref):
    lr, li = lr_ref[0], li_ref[0]
    dt = jnp.exp(ldt_ref[0])
    mag = jnp.exp(lr * dt)
    a_re, a_im = mag * jnp.cos(li * dt), mag * jnp.sin(li * dt)
    den = lr * lr + li * li
    nr, ni = a_re - 1.0, a_im
    f_re = (nr * lr + ni * li) / den
    f_im = (ni * lr - nr * li) / den
    return a_re, a_im, f_re, f_im


def _exact_transpose(x):
    k = x.shape[1]
    eye = jnp.where(lax.broadcasted_iota(jnp.int32, (k, k), 0) == lax.broadcasted_iota(jnp.int32, (k, k), 1),
                    1.0, 0.0).astype(BF16)

    def through_identity(piece):
        return lax.dot_general(eye, piece, (((1,), (1,)), ((), ())), preferred_element_type=F32)

    hi = x.astype(BF16)
    rest = x - hi.astype(F32)
    mid = rest.astype(BF16)
    lo = (rest - mid.astype(F32)).astype(BF16)
    return through_identity(hi) + through_identity(mid) + through_identity(lo)


def _s5_fill_weights(f_re, f_im, bre_ref, bim_ref, cre_ref, cim_ref, bb_ref, cc_ref):
    def over_groups(v):
        return jnp.concatenate([v] * S5_GB, axis=0)

    bt_re, bt_im = _exact_transpose(bre_ref[...]), _exact_transpose(bim_ref[...])
    same = (lax.broadcasted_iota(jnp.int32, (S5_CH, S5_ST), 0) // SSM_GROUP
            == lax.broadcasted_iota(jnp.int32, (S5_CH, S5_ST), 1) // SSM_STATE)
    bb_ref[:, 0:S5_ST] = jnp.where(same, over_groups(f_re * bt_re - f_im * bt_im), 0.0).astype(BF16)
    bb_ref[:, S5_ST:2 * S5_ST] = jnp.where(same, over_groups(f_re * bt_im + f_im * bt_re), 0.0).astype(BF16)

    ct_re, ct_im = _exact_transpose(cre_ref[...]), _exact_transpose(cim_ref[...])
    same = (lax.broadcasted_iota(jnp.int32, (S5_ST, S5_CH), 0) // SSM_STATE
            == lax.broadcasted_iota(jnp.int32, (S5_ST, S5_CH), 1) // SSM_GROUP)
    cc_ref[0:S5_ST, :] = jnp.where(same, over_groups(ct_re), 0.0).astype(BF16)
    cc_ref[S5_ST:2 * S5_ST, :] = jnp.where(same, over_groups(-ct_im), 0.0).astype(BF16)


def _s5_prompt_kernel(u_ref, lr_ref, li_ref, ldt_ref, bdre_ref, bdim_ref, cdre_ref, cdim_ref, d_ref,
                      sg_ref, sgb_ref, hre_ref, him_ref,
                      bb_ref, cc_ref, ul_ref, sgl_ref, lhs_ref, bu_ref, hch_ref):
    a_re, a_im, f_re, f_im = _s5_discretize(lr_ref, li_ref, ldt_ref)

    @pl.when(pl.program_id(1) == 0)
    def _():
        _s5_fill_weights(f_re, f_im, bdre_ref, bdim_ref, cdre_ref, cdim_ref, bb_ref, cc_ref)

    are8 = jnp.broadcast_to(a_re, (S5_SEG, S5_ST))
    aim8 = jnp.broadcast_to(a_im, (S5_SEG, S5_ST))
    d = d_ref[0]
    for hh in range(S5_LH):
        ul_ref[hh] = u_ref[:, hh * V7X_LANES:(hh + 1) * V7X_LANES]

    def chunk_rows(c):
        return slice(c * S5_CR, (c + 1) * S5_CR)

    def project_in(c):
        for tl in range(S5_TC):
            t = c * S5_TC + tl
            for hh in range(S5_LH):
                lhs_ref[t * S5_SEG:(t + 1) * S5_SEG, hh * V7X_LANES:(hh + 1) * V7X_LANES] = (
                    ul_ref[hh, pl.ds(t, S5_SEG, stride=S5_SEGLEN), :])
        bu_ref[chunk_rows(c), :] = _dot(lhs_ref[chunk_rows(c), :].astype(BF16), bb_ref[...])

    def scan(c, hr, hi, keep):
        for tl in range(S5_TC):
            t = c * S5_TC + tl
            rows = slice(t * S5_SEG, (t + 1) * S5_SEG)
            br = bu_ref[rows, 0:S5_ST]
            bi = bu_ref[rows, S5_ST:2 * S5_ST]
            hr, hi = are8 * hr - aim8 * hi + br, are8 * hi + aim8 * hr + bi
            if keep:
                crow = slice(tl * S5_SEG, (tl + 1) * S5_SEG)
                hch_ref[c % 2, crow, 0:S5_ST] = hr
                hch_ref[c % 2, crow, S5_ST:2 * S5_ST] = hi
        return hr, hi

    def project_out(c):
        y = _dot(hch_ref[c % 2].astype(BF16), cc_ref[...]) + d * lhs_ref[chunk_rows(c), :]
        lhs_ref[chunk_rows(c), :] = jax.nn.gelu(y)
        for tl in range(S5_TC):
            t = c * S5_TC + tl
            for hh in range(S5_LH):
                sgl_ref[hh, pl.ds(t, S5_SEG, stride=S5_SEGLEN), :] = (
                    lhs_ref[t * S5_SEG:(t + 1) * S5_SEG, hh * V7X_LANES:(hh + 1) * V7X_LANES])

    zeros = jnp.zeros((S5_SEG, S5_ST), F32)
    e_re, e_im = zeros, zeros
    project_in(0)
    for c in range(S5_NC):
        if c + 1 < S5_NC:
            project_in(c + 1)
        e_re, e_im = scan(c, e_re, e_im, keep=False)

    p_re, p_im = a_re, a_im
    for _ in range(int(math.log2(S5_SEGLEN))):
        p_re, p_im = p_re * p_re - p_im * p_im, 2.0 * p_re * p_im
    seg = lax.broadcasted_iota(jnp.int32, (S5_SEG, S5_ST), 0)
    qr = qi = jnp.zeros((1, S5_ST), F32)
    h_re = h_im = zeros
    for s in range(1, S5_SEG):
        qr, qi = (p_re * qr - p_im * qi + e_re[s - 1:s, :],
                  p_re * qi + p_im * qr + e_im[s - 1:s, :])
        h_re = jnp.where(seg == s, qr, h_re)
        h_im = jnp.where(seg == s, qi, h_im)

    for c in range(S5_NC):
        h_re, h_im = scan(c, h_re, h_im, keep=True)
        if c > 0:
            project_out(c - 1)
    project_out(S5_NC - 1)

    hre_ref[0] = h_re[S5_SEG - 1:S5_SEG, :]
    him_ref[0] = h_im[S5_SEG - 1:S5_SEG, :]
    for hh in range(S5_LH):
        sg_ref[:, hh * V7X_LANES:(hh + 1) * V7X_LANES] = sgl_ref[hh]
        sgb_ref[:, hh * V7X_LANES:(hh + 1) * V7X_LANES] = sgl_ref[hh].astype(BF16)


def _s5_param_specs(idx):
    def vec(width):
        return pl.BlockSpec((1, 1, width), lambda *g: (idx(*g), 0, 0))

    def rows(shape):
        return pl.BlockSpec(shape, lambda *g: (idx(*g), 0))

    return [vec(S5_ST), vec(S5_ST), vec(S5_ST),
            rows((S5_ST, SSM_GROUP)), rows((S5_ST, SSM_GROUP)),
            rows((S5_CH, SSM_STATE)), rows((S5_CH, SSM_STATE)), vec(S5_CH)]


def s5_prompt(u, s5p):
    return pl.pallas_call(
        _s5_prompt_kernel,
        out_shape=(jax.ShapeDtypeStruct((P_ROWS, SSM_WIDTH), F32),
                   jax.ShapeDtypeStruct((P_ROWS, SSM_WIDTH), BF16),
                   jax.ShapeDtypeStruct((BATCH, 1, SSM_GROUPS * SSM_STATE), F32),
                   jax.ShapeDtypeStruct((BATCH, 1, SSM_GROUPS * SSM_STATE), F32)),
        grid=(S5_NB, BATCH),
        in_specs=[pl.BlockSpec((SEQ, S5_CH), lambda j, b: (b, j))] + _s5_param_specs(lambda j, b: j),
        out_specs=(pl.BlockSpec((SEQ, S5_CH), lambda j, b: (b, j)),
                   pl.BlockSpec((SEQ, S5_CH), lambda j, b: (b, j)),
                   pl.BlockSpec((1, 1, S5_ST), lambda j, b: (b, 0, j)),
                   pl.BlockSpec((1, 1, S5_ST), lambda j, b: (b, 0, j))),
        scratch_shapes=[
            pltpu.VMEM((S5_CH, 2 * S5_ST), BF16),
            pltpu.VMEM((2 * S5_ST, S5_CH), BF16),
            pltpu.VMEM((S5_LH, SEQ, V7X_LANES), F32),
            pltpu.VMEM((S5_LH, SEQ, V7X_LANES), F32),
            pltpu.VMEM((SEQ, S5_CH), F32),
            pltpu.VMEM((SEQ, 2 * S5_ST), F32),
            pltpu.VMEM((2, S5_CR, 2 * S5_ST), F32),
        ],
        compiler_params=_params(2),
        name="s5_prompt",
    )(u, *s5p)


def _s5_sample_kernel(u_ref, h0re_ref, h0im_ref, lr_ref, li_ref, ldt_ref, bdre_ref, bdim_ref,
                      cdre_ref, cdim_ref, d_ref, sg_ref, sgb_ref, hre_ref, him_ref, bb_ref, cc_ref):
    a_re, a_im, f_re, f_im = _s5_discretize(lr_ref, li_ref, ldt_ref)
    _s5_fill_weights(f_re, f_im, bdre_ref, bdim_ref, cdre_ref, cdim_ref, bb_ref, cc_ref)
    u = u_ref[...]
    bu = _dot(u.astype(BF16), bb_ref[...])
    h0r, h0i = h0re_ref[...], h0im_ref[...]
    hr = a_re * h0r - a_im * h0i + bu[:, 0:S5_ST]
    hi = a_re * h0i + a_im * h0r + bu[:, S5_ST:2 * S5_ST]
    hre_ref[...] = hr
    him_ref[...] = hi
    y = _dot(hr.astype(BF16), cc_ref[0:S5_ST, :]) + _dot(hi.astype(BF16), cc_ref[S5_ST:2 * S5_ST, :])
    sg = jax.nn.gelu(y + d_ref[0] * u)
    sg_ref[...] = sg
    sgb_ref[...] = sg.astype(BF16)


def s5_sample(u, h0_re, h0_im, s5p):
    st = pl.BlockSpec((S_ROWS, S5_ST), lambda j: (0, j))
    return pl.pallas_call(
        _s5_sample_kernel,
        out_shape=(jax.ShapeDtypeStruct((S_ROWS, SSM_WIDTH), F32),
                   jax.ShapeDtypeStruct((S_ROWS, SSM_WIDTH), BF16),
                   jax.ShapeDtypeStruct((S_ROWS, SSM_GROUPS * SSM_STATE), F32),
                   jax.ShapeDtypeStruct((S_ROWS, SSM_GROUPS * SSM_STATE), F32)),
        grid=(S5_NB,),
        in_specs=[pl.BlockSpec((S_ROWS, S5_CH), lambda j: (0, j)), st, st]
        + _s5_param_specs(lambda j: j),
        out_specs=(pl.BlockSpec((S_ROWS, S5_CH), lambda j: (0, j)),
                   pl.BlockSpec((S_ROWS, S5_CH), lambda j: (0, j)), st, st),
        scratch_shapes=[
            pltpu.VMEM((S5_CH, 2 * S5_ST), BF16),
            pltpu.VMEM((2 * S5_ST, S5_CH), BF16),
        ],
        compiler_params=_params(1),
        name="s5_sample",
    )(u, h0_re, h0_im, *s5p)


def s5_block_params(lam_re, lam_im, log_dt, b_re, b_im, c_re, c_im, d_skip):
    n_state = SSM_GROUPS * SSM_STATE
    return (lam_re.reshape(S5_NB, 1, S5_ST), lam_im.reshape(S5_NB, 1, S5_ST),
            jnp.repeat(log_dt, SSM_STATE).reshape(S5_NB, 1, S5_ST),
            b_re.reshape(n_state, SSM_GROUP), b_im.reshape(n_state, SSM_GROUP),
            c_re.reshape(SSM_WIDTH, SSM_STATE), c_im.reshape(SSM_WIDTH, SSM_STATE),
            d_skip.reshape(S5_NB, 1, S5_CH))


SSM_GLU_TN = 512


def _ssm_glu_kernel(ap_hbm, as_ref, w_ref, tp_ref, ts_ref, op_ref, os_ref, wc_ref, r_ref, ap_ref, sem):
    tile = _RowTile([ap_hbm], [ap_ref], [sem], SSM_WIDTH // SSM_GLU_TN)

    def body(col):
        wc_ref[...] = w_ref[...].astype(BF16)

        def mm(k):
            r_ref[_sub_rows(k), :] = _dot(ap_ref[_sub_rows(k), :], wc_ref[...])

        def epi(k):
            op_ref[_sub_rows(k), :] = (
                tp_ref[_sub_rows(k), :] * jax.nn.sigmoid(r_ref[_sub_rows(k), :])).astype(op_ref.dtype)

        _interleave(NSUB, tile.mm_hooks(col, mm), epi)

        @_on_last_row_tile
        def _():
            z = _dot(as_ref[...], wc_ref[...])
            os_ref[...] = (ts_ref[...] * jax.nn.sigmoid(z)).astype(os_ref.dtype)

    tile.run(body)


def ssm_glu(sg_p, sg_s, sgb_p, sgb_s, w_glu):
    tn = SSM_GLU_TN
    return pl.pallas_call(
        _ssm_glu_kernel,
        out_shape=(jax.ShapeDtypeStruct((P_ROWS, SSM_WIDTH), BF16),
                   jax.ShapeDtypeStruct((S_ROWS, SSM_WIDTH), BF16)),
        grid=(NPT, SSM_WIDTH // tn),
        in_specs=[
            _ANY,
            _resident((S_ROWS, SSM_WIDTH), lambda i, j: (0, 0)),
            pl.BlockSpec((SSM_WIDTH, tn), lambda i, j: (0, j)),
            pl.BlockSpec((TM, tn), lambda i, j: (i, j)),
            pl.BlockSpec((S_ROWS, tn), lambda i, j: (0, j)),
        ],
        out_specs=(pl.BlockSpec((TM, tn), lambda i, j: (i, j)),
                   pl.BlockSpec((S_ROWS, tn), _sample_cols)),
        scratch_shapes=[pltpu.VMEM((SSM_WIDTH, tn), BF16), pltpu.VMEM((TM, tn), F32)]
        + _row_tile_scratch([SSM_WIDTH]),
        compiler_params=_params(2),
        name="ssm_glu",
    )(sgb_p, sgb_s, w_glu, sg_p, sg_s)


MERGE_TN = 256


def _merge_kernel(cp_hbm, cs_ref, yp_hbm, ys_ref, wa_ref, wb_ref, gap_ref, gbp_ref, gas_ref, gbs_ref,
                  op_ref, os_ref, wc_ref, r_ref, cp_ref, yp_ref, csem, ysem):
    tile = _RowTile([cp_hbm, yp_hbm], [cp_ref, yp_ref], [csem, ysem], D_MODEL // MERGE_TN)

    def body(col):
        wc_ref[0] = wa_ref[...].astype(BF16)
        wc_ref[1] = wb_ref[...].astype(BF16)

        def mm(k):
            r_ref[0, _sub_rows(k), :] = _dot(cp_ref[_sub_rows(k), :], wc_ref[0])
            r_ref[1, _sub_rows(k), :] = _dot(yp_ref[_sub_rows(k), :], wc_ref[1])

        def epi(k):
            rows = _sub_rows(k)
            op_ref[rows, :] = (gap_ref[rows, :] * r_ref[0, rows, :]
                               + gbp_ref[rows, :] * r_ref[1, rows, :]).astype(op_ref.dtype)

        _interleave(NSUB, tile.mm_hooks(col, mm), epi)

        @_on_last_row_tile
        def _():
            ya = _dot(cs_ref[...], wc_ref[0])
            yb = _dot(ys_ref[...], wc_ref[1])
            os_ref[...] = (gas_ref[...] * ya + gbs_ref[...] * yb).astype(os_ref.dtype)

    tile.run(body)


def merge(c_p, c_s, yg_p, yg_s, w_conv_out, w_ssm_out, gates_p, gates_s):
    tn = MERGE_TN
    nj = D_MODEL // tn
    return pl.pallas_call(
        _merge_kernel,
        out_shape=(jax.ShapeDtypeStruct((P_ROWS, D_MODEL), BF16),
                   jax.ShapeDtypeStruct((S_ROWS, D_MODEL), BF16)),
        grid=(NPT, nj),
        in_specs=[
            _ANY,
            _resident((S_ROWS, CONV_WIDTH), lambda i, j: (0, 0)),
            _ANY,
            _resident((S_ROWS, SSM_WIDTH), lambda i, j: (0, 0)),
            pl.BlockSpec((CONV_WIDTH, tn), lambda i, j: (0, j)),
            pl.BlockSpec((SSM_WIDTH, tn), lambda i, j: (0, j)),
            pl.BlockSpec((TM, tn), lambda i, j: (i, j)),
            pl.BlockSpec((TM, tn), lambda i, j: (i, j + nj)),
            pl.BlockSpec((S_ROWS, tn), lambda i, j: (0, j)),
            pl.BlockSpec((S_ROWS, tn), lambda i, j: (0, j + nj)),
        ],
        out_specs=(pl.BlockSpec((TM, tn), lambda i, j: (i, j)),
                   pl.BlockSpec((S_ROWS, tn), _sample_cols)),
        scratch_shapes=[pltpu.VMEM((2, CONV_WIDTH, tn), BF16), pltpu.VMEM((2, TM, tn), F32)]
        + _row_tile_scratch([CONV_WIDTH, SSM_WIDTH]),
        compiler_params=_params(2),
        name="merge",
    )(c_p, c_s, yg_p, yg_s, w_conv_out, w_ssm_out, gates_p, gates_p, gates_s, gates_s)


OPROJ_TN = 512


def _oproj_kernel(ap_hbm, as_ref, w_ref, xp_ref, xs_ref, op_ref, os_ref, wc_ref, ap_ref, sem):
    tile = _RowTile([ap_hbm], [ap_ref], [sem], D_MODEL // OPROJ_TN)

    def body(col):
        wc_ref[...] = w_ref[...].astype(BF16)

        def mm(k):
            op_ref[_sub_rows(k), :] = _dot(ap_ref[_sub_rows(k), :], wc_ref[...])

        def epi(k):
            op_ref[_sub_rows(k), :] = op_ref[_sub_rows(k), :] + xp_ref[_sub_rows(k), :]

        _interleave(NSUB, tile.mm_hooks(col, mm), epi)

        @_on_last_row_tile
        def _():
            os_ref[...] = xs_ref[...] + _dot(as_ref[...], wc_ref[...])

    tile.run(body)


def oproj(merged_p, merged_s, w_o, xp, xs):
    tn = OPROJ_TN
    return pl.pallas_call(
        _oproj_kernel,
        out_shape=(jax.ShapeDtypeStruct((P_ROWS, D_MODEL), F32),
                   jax.ShapeDtypeStruct((S_ROWS, D_MODEL), F32)),
        grid=(NPT, D_MODEL // tn),
        in_specs=[
            _ANY,
            _resident((S_ROWS, D_MODEL), lambda i, j: (0, 0)),
            pl.BlockSpec((D_MODEL, tn), lambda i, j: (0, j)),
            pl.BlockSpec((TM, tn), lambda i, j: (i, j)),
            pl.BlockSpec((S_ROWS, tn), lambda i, j: (0, j)),
        ],
        out_specs=(pl.BlockSpec((TM, tn), lambda i, j: (i, j)),
                   pl.BlockSpec((S_ROWS, tn), _sample_cols)),
        scratch_shapes=[pltpu.VMEM((D_MODEL, tn), BF16)] + _row_tile_scratch([D_MODEL]),
        compiler_params=_params(2),
        name="oproj",
    )(merged_p, merged_s, w_o, xp, xs)


DOWN_SUBS = (128, 128, 128, 128)
DOWN_TM = sum(DOWN_SUBS)
DOWN_NPT = P_ROWS // DOWN_TM
DOWN_TN = 512
DOWN_NORM_ROWS = 64


def _ffn_down_kernel(ap_hbm, as_ref, w_ref, xp_ref, xs_ref, g_ref, yp_ref, os_ref, ap_ref, sem):
    tile = _RowTile([ap_hbm], [ap_ref], [sem], D_MODEL // DOWN_TN, sizes=DOWN_SUBS)
    cols = pl.ds(pl.multiple_of(pl.program_id(1) * DOWN_TN, DOWN_TN), DOWN_TN)

    def body(col):
        def mm(k):
            rows = tile.rows(k)
            yp_ref[rows, cols] = xp_ref[rows, :] + _dot(ap_ref[rows, :], w_ref[...])

        if col == "mid":
            yp_ref[:, cols] = xp_ref[...] + _dot(ap_ref[...], w_ref[...])
        else:
            for k in range(len(DOWN_SUBS)):
                tile.mm_hooks(col, mm)(k)

        @pl.when(pl.program_id(0) == DOWN_NPT - 1)
        def _():
            os_ref[...] = xs_ref[...] + _dot(as_ref[...], w_ref[...])

        if col == "last":
            def norm(c, carry):
                rows = pl.ds(pl.multiple_of(c * DOWN_NORM_ROWS, DOWN_NORM_ROWS), DOWN_NORM_ROWS)
                yp_ref[rows, :] = _rms(yp_ref[rows, :], g_ref[...])
                return carry

            lax.fori_loop(0, DOWN_TM // DOWN_NORM_ROWS, norm, 0)

    tile.run(body)


def ffn_down(h_p, h_s, w_down_bf16, x1_p, x1_s, final_g):
    tn = DOWN_TN
    last = DOWN_NPT - 1
    return pl.pallas_call(
        _ffn_down_kernel,
        out_shape=(jax.ShapeDtypeStruct((P_ROWS, D_MODEL), F32),
                   jax.ShapeDtypeStruct((S_ROWS, D_MODEL), F32)),
        grid=(DOWN_NPT, D_MODEL // tn),
        in_specs=[
            _ANY,
            _resident((S_ROWS, D_FF), lambda i, j: (0, 0)),
            pl.BlockSpec((D_FF, tn), lambda i, j: (0, j)),
            pl.BlockSpec((DOWN_TM, tn), lambda i, j: (i, j)),
            pl.BlockSpec((S_ROWS, tn), lambda i, j: (0, j)),
            pl.BlockSpec((1, D_MODEL), lambda i, j: (0, 0)),
        ],
        out_specs=(pl.BlockSpec((DOWN_TM, D_MODEL), lambda i, j: (i, 0)),
                   pl.BlockSpec((S_ROWS, tn), lambda i, j: (0, jnp.where(i == last, j, 0)))),
        scratch_shapes=_row_tile_scratch([D_FF], sizes=DOWN_SUBS),
        compiler_params=_params(2),
        name="ffn_down",
    )(h_p, h_s, w_down_bf16, x1_p, x1_s, final_g)


FFN_TN = 256
FFN_NJ = D_FF // FFN_TN
FFN_PAD = V7X_SUBLANES


def _ffn_up_kernel(ap_hbm, as_ref, wg_ref, wv_ref, s0_ref, s1_ref, cw_ref, cb_ref, wd_ref,
                   hp_ref, hs_ref, tail_ref, gs_ref, wdb_ref, wc_ref, g_ref, v_ref, ap_ref, sem):
    tn = FFN_TN
    tile = _RowTile([ap_hbm], [ap_ref], [sem], FFN_NJ)

    def body(col):
        wc_ref[:, 0:tn] = wg_ref[...].astype(BF16)
        wc_ref[:, tn:2 * tn] = wv_ref[...].astype(BF16)
        wdb_ref[...] = wd_ref[...].astype(BF16)
        cols = pl.ds(pl.multiple_of(pl.program_id(1) * tn, tn), tn)
        w0, w1, w2 = cw_ref[0:1, cols], cw_ref[1:2, cols], cw_ref[2:3, cols]
        cb = cb_ref[:, cols]
        g_ref[0:FFN_PAD, :] = jnp.zeros((FFN_PAD, tn), F32)

        def mm(k):
            r = _dot(ap_ref[_sub_rows(k), :], wc_ref[...])
            g_ref[_sub_rows(k, FFN_PAD), :] = r[:, 0:tn]
            v_ref[_sub_rows(k), :] = r[:, tn:2 * tn]

        def epi(k):
            gc = (w0 * g_ref[_sub_rows(k, FFN_PAD - 2), :] + w1 * g_ref[_sub_rows(k, FFN_PAD - 1), :]
                  + w2 * g_ref[_sub_rows(k, FFN_PAD), :] + cb)
            hp_ref[_sub_rows(k), :] = (gc * jax.nn.sigmoid(gc) * v_ref[_sub_rows(k), :]).astype(hp_ref.dtype)

        _interleave(NSUB, tile.mm_hooks(col, mm), epi)
        tail_ref[0, :, cols] = g_ref[TM:TM + FFN_PAD, :]

        @_on_last_row_tile
        def _():
            r = _dot(as_ref[...], wc_ref[...])
            gate, val = r[:, 0:tn], r[:, tn:2 * tn]
            gc = w0 * s0_ref[...] + w1 * s1_ref[...] + w2 * gate + cb
            hs_ref[...] = (gc * jax.nn.sigmoid(gc) * val).astype(hs_ref.dtype)
            gs_ref[...] = gate

    tile.run(body)


def ffn_up(xn2_p, xn2_s, w_up, ffn_old, ffn_conv_w, ffn_conv_b, w_down):
    nj = FFN_NJ
    tn = FFN_TN
    wd_rows = D_FF // (NPT * nj)

    def wd_slab(i, j):
        return (i * nj + j, 0)

    whole_row = lambda i, j: (0, 0)

    return pl.pallas_call(
        _ffn_up_kernel,
        out_shape=(jax.ShapeDtypeStruct((P_ROWS, D_FF), BF16),
                   jax.ShapeDtypeStruct((S_ROWS, D_FF), BF16),
                   jax.ShapeDtypeStruct((NPT, FFN_PAD, D_FF), F32),
                   jax.ShapeDtypeStruct((S_ROWS, D_FF), F32),
                   jax.ShapeDtypeStruct((D_FF, D_MODEL), BF16)),
        grid=(NPT, nj),
        in_specs=[
            _ANY,
            _resident((S_ROWS, D_MODEL), lambda i, j: (0, 0)),
            pl.BlockSpec((D_MODEL, tn), lambda i, j: (0, j)),
            pl.BlockSpec((D_MODEL, tn), lambda i, j: (0, j + nj)),
            pl.BlockSpec((S_ROWS, tn), _sample_cols),
            pl.BlockSpec((S_ROWS, tn), _sample_cols),
            pl.BlockSpec((FFN_K, D_FF), whole_row),
            pl.BlockSpec((1, D_FF), whole_row),
            pl.BlockSpec((wd_rows, D_MODEL), wd_slab),
        ],
        out_specs=(
            pl.BlockSpec((TM, tn), lambda i, j: (i, j)),
            pl.BlockSpec((S_ROWS, tn), _sample_cols),
            pl.BlockSpec((1, FFN_PAD, D_FF), lambda i, j: (i, 0, 0)),
            pl.BlockSpec((S_ROWS, tn), _sample_cols),
            pl.BlockSpec((wd_rows, D_MODEL), wd_slab),
        ),
        scratch_shapes=[
            pltpu.VMEM((D_MODEL, 2 * tn), BF16),
            pltpu.VMEM((FFN_PAD + TM, tn), F32),
            pltpu.VMEM((TM, tn), F32),
        ] + _row_tile_scratch([D_MODEL]),
        compiler_params=_params(2),
        name="ffn_up",
    )(xn2_p, xn2_s, w_up, w_up, ffn_old[0], ffn_old[1], ffn_conv_w, ffn_conv_b, w_down)


def kernel(x_prompt, x_sample, state_conv, state_ssm_re, state_ssm_im, state_ffn_conv,
           norm_mix_g, w_in, conv_w, conv_b, ln_g, ln_b, w_conv_out,
           lam_re, lam_im, log_dt, b_re, b_im, c_re, c_im, d_skip, w_glu, w_ssm_out, w_o,
           norm_ffn_g, w_up, ffn_conv_w, ffn_conv_b, w_down, final_norm_g):
    xp = x_prompt.reshape(P_ROWS, D_MODEL)
    xs = x_sample.reshape(S_ROWS, D_MODEL)

    def row(v):
        return v.reshape(1, -1)

    xn_p, xn_s = rownorm_pair(xp, xs, row(norm_mix_g[0]), BF16)
    glu_p, glu_s = inproj_glu(xn_p, xn_s, w_in[0])
    u_p, u_s = inproj_cols(xn_p, xn_s, w_in[0], 2 * CONV_WIDTH, SSM_WIDTH, False, "inproj_ssm")
    gates_p, gates_s = inproj_cols(xn_p, xn_s, w_in[0], 2 * CONV_WIDTH + SSM_WIDTH, 2 * D_MODEL, True,
                                   "inproj_gates")

    conv_vecs = (conv_w[0], row(conv_b[0]), row(ln_g[0]), row(ln_b[0]))
    c_p, conv_p = conv_prompt(glu_p, *conv_vecs)
    c_s, conv_s = conv_sample(glu_s, state_conv[0].transpose(1, 0, 2), *conv_vecs)
    conv_s = conv_s.transpose(1, 0, 2)

    s5p = s5_block_params(lam_re[0], lam_im[0], log_dt[0], b_re[0], b_im[0], c_re[0], c_im[0], d_skip[0])
    n_state = SSM_GROUPS * SSM_STATE
    sg_p, sgb_p, ssr_p, ssi_p = s5_prompt(u_p, s5p)
    sg_s, sgb_s, ssr_s, ssi_s = s5_sample(u_s, state_ssm_re[0].reshape(S_ROWS, n_state),
                                          state_ssm_im[0].reshape(S_ROWS, n_state), s5p)
    yg_p, yg_s = ssm_glu(sg_p, sg_s, sgb_p, sgb_s, w_glu[0])

    merged_p, merged_s = merge(c_p, c_s, yg_p, yg_s, w_conv_out[0], w_ssm_out[0], gates_p, gates_s)
    x1_p, x1_s = oproj(merged_p, merged_s, w_o[0], xp, xs)

    xn2_p, xn2_s = rownorm_pair(x1_p, x1_s, row(norm_ffn_g[0]), BF16)
    ffn_old = [state_ffn_conv[0, :, k, :] for k in range(FFN_K - 1)]
    h_p, h_s, gate_tail, gate_s, w_down_bf16 = ffn_up(
        xn2_p, xn2_s, w_up[0], ffn_old, ffn_conv_w[0], row(ffn_conv_b[0]), w_down[0])
    y_p, x2_s = ffn_down(h_p, h_s, w_down_bf16, x1_p, x1_s, row(final_norm_g))
    y_s = rownorm(x2_s, row(final_norm_g), F32, S_ROWS)

    ffn_p = gate_tail[:, FFN_PAD - (FFN_K - 1):, :]
    ffn_s = jnp.stack([ffn_old[1], gate_s], axis=1)
    state_shape = (1, -1, SSM_GROUPS, SSM_STATE)
    return (y_p.reshape(BATCH, SEQ, D_MODEL), y_s.reshape(DEC_BATCH, 1, D_MODEL),
            conv_p[None], conv_s[None],
            ssr_p.reshape(state_shape), ssi_p.reshape(state_shape),
            ssr_s.reshape(state_shape), ssi_s.reshape(state_shape),
            ffn_p[None], ffn_s[None])
```

```python
import functools
import math

import jax
import jax.numpy as jnp
from jax import lax
from jax.experimental import pallas as pl
from jax.experimental.pallas import tpu as pltpu

D_MODEL = 4096
BATCH = 4
SEQ = 2048
DEC_BATCH = 128
CONV_WIDTH = D_MODEL // 2
CONV_K = 31
SSM_WIDTH = D_MODEL // 2
SSM_GROUP = 16
SSM_GROUPS = SSM_WIDTH // SSM_GROUP
SSM_STATE = 64
D_FF = 11008
FFN_K = 3
EPS = 1e-6

P_ROWS = BATCH * SEQ
S_ROWS = DEC_BATCH

V7X_SUBLANES = 8
V7X_LANES = 128
V7X_VMEM_LIMIT = 58 * 1024 * 1024

TM = SEQ
NPT = P_ROWS // TM
SUB_SIZES = (768, 768, 384, 128)
SUB_STARTS = tuple(sum(SUB_SIZES[:k]) for k in range(len(SUB_SIZES)))
NSUB = len(SUB_SIZES)
assert sum(SUB_SIZES) == TM

BF16 = jnp.bfloat16
F32 = jnp.float32


def _params(n_grid_dims):
    return pltpu.CompilerParams(
        dimension_semantics=("arbitrary",) * n_grid_dims,
        vmem_limit_bytes=V7X_VMEM_LIMIT)


def _dot(a, b):
    return jnp.dot(a, b, preferred_element_type=F32)


def _rms(x, g):
    return x * lax.rsqrt(jnp.mean(x * x, axis=-1, keepdims=True) + EPS) * g


def _interleave(n, mm, epi):
    mm(0)
    for k in range(1, n):
        mm(k)
        epi(k - 1)
    epi(n - 1)


def _sub_rows(k, offset=0):
    return slice(offset + SUB_STARTS[k], offset + SUB_STARTS[k] + SUB_SIZES[k])


class _RowTile:
    def __init__(self, hbm_refs, bufs, sems, ncol, sizes=SUB_SIZES):
        self.hbm_refs, self.bufs, self.sems, self.ncol = hbm_refs, bufs, sems, ncol
        self.sizes = sizes
        self.starts = tuple(sum(sizes[:k]) for k in range(len(sizes)))
        self.tm = sum(sizes)
        self.i, self.j = pl.program_id(0), pl.program_id(1)

    def rows(self, k):
        return slice(self.starts[k], self.starts[k] + self.sizes[k])

    def _copies(self, tile, k):
        return [pltpu.make_async_copy(hbm.at[pl.ds(tile * self.tm + self.starts[k], self.sizes[k]), :],
                                      buf.at[pl.ds(self.starts[k], self.sizes[k]), :], sem.at[k])
                for hbm, buf, sem in zip(self.hbm_refs, self.bufs, self.sems)]

    def fetch_first_tile(self):
        @pl.when((self.i == 0) & (self.j == 0))
        def _():
            for k in range(len(self.sizes)):
                for c in self._copies(0, k):
                    c.start()

    def wait(self, k):
        for c in self._copies(self.i, k):
            c.wait()

    def fetch_next(self, k):
        @pl.when(self.i < P_ROWS // self.tm - 1)
        def _():
            for c in self._copies(self.i + 1, k):
                c.start()

    def run(self, body):
        assert self.ncol >= 2
        self.fetch_first_tile()
        pl.when(self.j == 0)(functools.partial(body, "first"))
        pl.when((self.j > 0) & (self.j < self.ncol - 1))(functools.partial(body, "mid"))
        pl.when(self.j == self.ncol - 1)(functools.partial(body, "last"))

    def mm_hooks(self, col, mm):
        def wrapped(k):
            if col == "first":
                self.wait(k)
            mm(k)
            if col == "last":
                self.fetch_next(k)
        return wrapped


_ANY = pl.BlockSpec(memory_space=pl.ANY)


def _row_tile_scratch(widths, sizes=SUB_SIZES):
    return ([pltpu.VMEM((sum(sizes), w), BF16) for w in widths]
            + [pltpu.SemaphoreType.DMA((len(sizes),)) for _ in widths])


def _resident(shape, index_map):
    return pl.BlockSpec(shape, index_map, pipeline_mode=pl.Buffered(1))


def _sample_cols(i, j):
    return (0, jnp.where(i == NPT - 1, j, 0))


def _on_last_row_tile(fn):
    pl.when(pl.program_id(0) == NPT - 1)(fn)


def _rownorm_kernel(x_ref, g_ref, o_ref):
    o_ref[...] = _rms(x_ref[...], g_ref[...]).astype(o_ref.dtype)


def rownorm(x, g, out_dtype, row_block):
    rows = x.shape[0]
    return pl.pallas_call(
        _rownorm_kernel,
        out_shape=jax.ShapeDtypeStruct((rows, D_MODEL), out_dtype),
        grid=(rows // row_block,),
        in_specs=[
            pl.BlockSpec((row_block, D_MODEL), lambda i: (i, 0)),
            pl.BlockSpec((1, D_MODEL), lambda i: (0, 0)),
        ],
        out_specs=pl.BlockSpec((row_block, D_MODEL), lambda i: (i, 0)),
        compiler_params=_params(1),
        name="rownorm",
    )(x, g)


def rownorm_pair(xp, xs, g, out_dtype):
    return rownorm(xp, g, out_dtype, 512), rownorm(xs, g, out_dtype, S_ROWS)


GLU_TN = 256


def _inproj_glu_kernel(ap_hbm, as_ref, wa_ref, wg_ref, op_ref, os_ref, wc_ref, r_ref, ap_ref, sem):
    tn = GLU_TN
    tile = _RowTile([ap_hbm], [ap_ref], [sem], CONV_WIDTH // tn)

    def body(col):
        wc_ref[:, 0:tn] = wa_ref[...].astype(BF16)
        wc_ref[:, tn:2 * tn] = wg_ref[...].astype(BF16)

        def mm(k):
            r_ref[_sub_rows(k), :] = _dot(ap_ref[_sub_rows(k), :], wc_ref[...])

        def epi(k):
            r = r_ref[_sub_rows(k), :]
            op_ref[_sub_rows(k), :] = r[:, 0:tn] * jax.nn.sigmoid(r[:, tn:2 * tn])

        _interleave(NSUB, tile.mm_hooks(col, mm), epi)

        @_on_last_row_tile
        def _():
            r = _dot(as_ref[...], wc_ref[...])
            os_ref[...] = r[:, 0:tn] * jax.nn.sigmoid(r[:, tn:2 * tn])

    tile.run(body)


def inproj_glu(xn_p, xn_s, w_in):
    tn = GLU_TN
    nj = CONV_WIDTH // tn
    return pl.pallas_call(
        _inproj_glu_kernel,
        out_shape=(jax.ShapeDtypeStruct((P_ROWS, CONV_WIDTH), F32),
                   jax.ShapeDtypeStruct((S_ROWS, CONV_WIDTH), F32)),
        grid=(NPT, nj),
        in_specs=[
            _ANY,
            _resident((S_ROWS, D_MODEL), lambda i, j: (0, 0)),
            pl.BlockSpec((D_MODEL, tn), lambda i, j: (0, j)),
            pl.BlockSpec((D_MODEL, tn), lambda i, j: (0, j + nj)),
        ],
        out_specs=(pl.BlockSpec((TM, tn), lambda i, j: (i, j)),
                   pl.BlockSpec((S_ROWS, tn), _sample_cols)),
        scratch_shapes=[pltpu.VMEM((D_MODEL, 2 * tn), BF16), pltpu.VMEM((TM, 2 * tn), F32)]
        + _row_tile_scratch([D_MODEL]),
        compiler_params=_params(2),
        name="inproj_glu",
    )(xn_p, xn_s, w_in, w_in)


def _inproj_cols_kernel(ap_hbm, as_ref, w_ref, op_ref, os_ref, wc_ref, ap_ref, sem, *, squash, ncol):
    tile = _RowTile([ap_hbm], [ap_ref], [sem], ncol)

    def body(col):
        wc_ref[...] = w_ref[...].astype(BF16)

        def mm(k):
            op_ref[_sub_rows(k), :] = _dot(ap_ref[_sub_rows(k), :], wc_ref[...])

        def epi(k):
            if squash:
                op_ref[_sub_rows(k), :] = jax.nn.sigmoid(op_ref[_sub_rows(k), :])

        _interleave(NSUB, tile.mm_hooks(col, mm), epi)

        @_on_last_row_tile
        def _():
            r = _dot(as_ref[...], wc_ref[...])
            os_ref[...] = jax.nn.sigmoid(r) if squash else r

    tile.run(body)


def inproj_cols(xn_p, xn_s, w_in, col0, width, squash, name):
    tn = 512
    return pl.pallas_call(
        functools.partial(_inproj_cols_kernel, squash=squash, ncol=width // tn),
        out_shape=(jax.ShapeDtypeStruct((P_ROWS, width), F32),
                   jax.ShapeDtypeStruct((S_ROWS, width), F32)),
        grid=(NPT, width // tn),
        in_specs=[
            _ANY,
            _resident((S_ROWS, D_MODEL), lambda i, j: (0, 0)),
            pl.BlockSpec((D_MODEL, tn), lambda i, j: (0, j + col0 // tn)),
        ],
        out_specs=(pl.BlockSpec((TM, tn), lambda i, j: (i, j)),
                   pl.BlockSpec((S_ROWS, tn), _sample_cols)),
        scratch_shapes=[pltpu.VMEM((D_MODEL, tn), BF16)] + _row_tile_scratch([D_MODEL]),
        compiler_params=_params(2),
        name=name,
    )(xn_p, xn_s, w_in)


CONV_TT = 256
CONV_HALO = 32
CONV_RC = 32
CONV_LC = 256
CONV_PIECE = 32


def _ln_swish(c, g, b):
    mu = jnp.mean(c, axis=-1, keepdims=True)
    d = c - mu
    var = jnp.mean(d * d, axis=-1, keepdims=True)
    r = d * lax.rsqrt(var + EPS) * g + b
    return r * jax.nn.sigmoid(r)


def _conv_prompt_kernel(x_ref, w_ref, cb_ref, g_ref, b_ref, o_ref, st_ref, xs_ref, cbuf_ref, w8_ref):
    t = pl.program_id(1)
    nt = pl.num_programs(1)
    sub = V7X_SUBLANES

    @pl.when(t == 0)
    def _():
        xs_ref[0, 0:CONV_HALO, :] = jnp.zeros((CONV_HALO, CONV_WIDTH), F32)
        xs_ref[0, CONV_HALO + CONV_TT:CONV_HALO + CONV_TT + sub, :] = jnp.zeros((sub, CONV_WIDTH), F32)
        for k in range(CONV_K):
            w8_ref[k] = jnp.broadcast_to(w_ref[k:k + 1, :], (sub, CONV_WIDTH))

    xs_ref[0, CONV_HALO:CONV_HALO + CONV_TT, :] = x_ref[...]

    def shift_piece(p, carry):
        r = pl.multiple_of(p * CONV_PIECE, CONV_PIECE)
        piece = xs_ref[0, pl.ds(r, CONV_PIECE + sub), :]
        for m in range(1, sub):
            rolled = pltpu.roll(piece, CONV_PIECE + sub - m, axis=0)
            xs_ref[m, pl.ds(r, CONV_PIECE), :] = rolled[0:CONV_PIECE]
        return carry

    lax.fori_loop(0, (CONV_HALO + CONV_TT) // CONV_PIECE, shift_piece, 0)

    off = CONV_HALO - (CONV_K - 1)

    lane_chunks = [slice(l0, l0 + CONV_LC) for l0 in range(0, CONV_WIDTH, CONV_LC)]

    def chunk(c, carry):
        r0 = pl.multiple_of(c * CONV_RC, CONV_RC)
        part = jnp.zeros((CONV_RC, CONV_LC), F32)
        for lanes in lane_chunks:
            accs = [jnp.zeros((sub, CONV_LC), F32) for _ in range(CONV_RC // sub)]
            for k in range(CONV_K):
                o = off + k
                wk = w8_ref[k, :, lanes]
                for rg in range(CONV_RC // sub):
                    win = xs_ref[o % sub, pl.ds(r0 + (o // sub + rg) * sub, sub), lanes]
                    accs[rg] = accs[rg] + win * wk
            acc = jnp.concatenate(accs, axis=0) + cb_ref[:, lanes]
            cbuf_ref[:, lanes] = acc
            part = part + acc
        mu = jnp.sum(part, axis=-1, keepdims=True) / CONV_WIDTH
        part = jnp.zeros((CONV_RC, CONV_LC), F32)
        for lanes in lane_chunks:
            d = cbuf_ref[:, lanes] - mu
            part = part + d * d
        rstd = lax.rsqrt(jnp.sum(part, axis=-1, keepdims=True) / CONV_WIDTH + EPS)
        for lanes in lane_chunks:
            r = (cbuf_ref[:, lanes] - mu) * rstd * g_ref[:, lanes] + b_ref[:, lanes]
            o_ref[pl.ds(r0, CONV_RC), lanes] = (r * jax.nn.sigmoid(r)).astype(o_ref.dtype)
        return carry

    lax.fori_loop(0, CONV_TT // CONV_RC, chunk, 0)

    @pl.when(t == nt - 1)
    def _():
        st_ref[0] = xs_ref[off, CONV_TT:CONV_TT + CONV_K - 1, :]

    xs_ref[0, 0:CONV_HALO, :] = xs_ref[0, CONV_TT:CONV_TT + CONV_HALO, :]


def conv_prompt(glu, conv_w, conv_b, ln_g, ln_b):
    nt = SEQ // CONV_TT
    vec = pl.BlockSpec((1, CONV_WIDTH), lambda b, t: (0, 0))
    return pl.pallas_call(
        _conv_prompt_kernel,
        out_shape=(jax.ShapeDtypeStruct((P_ROWS, CONV_WIDTH), BF16),
                   jax.ShapeDtypeStruct((BATCH, CONV_K - 1, CONV_WIDTH), F32)),
        grid=(BATCH, nt),
        in_specs=[
            pl.BlockSpec((CONV_TT, CONV_WIDTH), lambda b, t: (b * nt + t, 0)),
            pl.BlockSpec((CONV_K, CONV_WIDTH), lambda b, t: (0, 0)),
            vec, vec, vec,
        ],
        out_specs=(pl.BlockSpec((CONV_TT, CONV_WIDTH), lambda b, t: (b * nt + t, 0)),
                   pl.BlockSpec((1, CONV_K - 1, CONV_WIDTH), lambda b, t: (b, 0, 0))),
        scratch_shapes=[
            pltpu.VMEM((V7X_SUBLANES, CONV_HALO + CONV_TT + V7X_SUBLANES, CONV_WIDTH), F32),
            pltpu.VMEM((CONV_RC, CONV_WIDTH), F32),
            pltpu.VMEM((CONV_K, V7X_SUBLANES, CONV_WIDTH), F32),
        ],
        compiler_params=_params(2),
        name="conv_prompt",
    )(glu, conv_w, conv_b, ln_g, ln_b)


CONV_SB = 16


def _conv_sample_kernel(x_ref, st_ref, w_ref, cb_ref, g_ref, b_ref, o_ref, nst_ref):
    nb = CONV_K - 1
    x = x_ref[...]
    acc = x * w_ref[nb:CONV_K, :] + cb_ref[...]
    for k in range(nb):
        acc = acc + st_ref[k] * w_ref[k:k + 1, :]
    o_ref[...] = _ln_swish(acc, g_ref[...], b_ref[...]).astype(o_ref.dtype)
    for k in range(nb - 1):
        nst_ref[k] = st_ref[k + 1]
    nst_ref[nb - 1] = x


def conv_sample(glu, state_tsc, conv_w, conv_b, ln_g, ln_b):
    vec = pl.BlockSpec((1, CONV_WIDTH), lambda i: (0, 0))
    st_spec = pl.BlockSpec((CONV_K - 1, CONV_SB, CONV_WIDTH), lambda i: (0, i, 0))
    return pl.pallas_call(
        _conv_sample_kernel,
        out_shape=(jax.ShapeDtypeStruct((S_ROWS, CONV_WIDTH), BF16),
                   jax.ShapeDtypeStruct((CONV_K - 1, S_ROWS, CONV_WIDTH), F32)),
        grid=(S_ROWS // CONV_SB,),
        in_specs=[
            pl.BlockSpec((CONV_SB, CONV_WIDTH), lambda i: (i, 0)),
            st_spec,
            pl.BlockSpec((CONV_K, CONV_WIDTH), lambda i: (0, 0)),
            vec, vec, vec,
        ],
        out_specs=(pl.BlockSpec((CONV_SB, CONV_WIDTH), lambda i: (i, 0)), st_spec),
        compiler_params=_params(1),
        name="conv_sample",
    )(glu, state_tsc, conv_w, conv_b, ln_g, ln_b)


S5_GB = 16
S5_CH = S5_GB * SSM_GROUP
S5_ST = S5_GB * SSM_STATE
S5_NB = SSM_GROUPS // S5_GB
S5_SEG = V7X_SUBLANES
S5_SEGLEN = SEQ // S5_SEG
S5_TC = 64
S5_NC = S5_SEGLEN // S5_TC
S5_CR = S5_TC * S5_SEG
S5_LH = S5_CH // V7X_LANES


def _s5_discretize(lr_ref, li_ref, ldt_ref):
    lr, li = lr_ref[0], li_ref[0]
    dt = jnp.exp(ldt_ref[0])
    mag = jnp.exp(lr * dt)
    a_re, a_im = mag * jnp.cos(li * dt), mag * jnp.sin(li * dt)
    den = lr * lr + li * li
    nr, ni = a_re - 1.0, a_im
    f_re = (nr * lr + ni * li) / den
    f_im = (ni * lr - nr * li) / den
    return a_re, a_im, f_re, f_im


def _exact_transpose(x):
    k = x.shape[1]
    eye = jnp.where(lax.broadcasted_iota(jnp.int32, (k, k), 0) == lax.broadcasted_iota(jnp.int32, (k, k), 1),
                    1.0, 0.0).astype(BF16)

    def through_identity(piece):
        return lax.dot_general(eye, piece, (((1,), (1,)), ((), ())), preferred_element_type=F32)

    hi = x.astype(BF16)
    rest = x - hi.astype(F32)
    mid = rest.astype(BF16)
    lo = (rest - mid.astype(F32)).astype(BF16)
    return through_identity(hi) + through_identity(mid) + through_identity(lo)


def _s5_fill_weights(f_re, f_im, bre_ref, bim_ref, cre_ref, cim_ref, bb_ref, cc_ref):
    def over_groups(v):
        return jnp.concatenate([v] * S5_GB, axis=0)

    bt_re, bt_im = _exact_transpose(bre_ref[...]), _exact_transpose(bim_ref[...])
    same = (lax.broadcasted_iota(jnp.int32, (S5_CH, S5_ST), 0) // SSM_GROUP
            == lax.broadcasted_iota(jnp.int32, (S5_CH, S5_ST), 1) // SSM_STATE)
    bb_ref[:, 0:S5_ST] = jnp.where(same, over_groups(f_re * bt_re - f_im * bt_im), 0.0).astype(BF16)
    bb_ref[:, S5_ST:2 * S5_ST] = jnp.where(same, over_groups(f_re * bt_im + f_im * bt_re), 0.0).astype(BF16)

    ct_re, ct_im = _exact_transpose(cre_ref[...]), _exact_transpose(cim_ref[...])
    same = (lax.broadcasted_iota(jnp.int32, (S5_ST, S5_CH), 0) // SSM_STATE
            == lax.broadcasted_iota(jnp.int32, (S5_ST, S5_CH), 1) // SSM_GROUP)
    cc_ref[0:S5_ST, :] = jnp.where(same, over_groups(ct_re), 0.0).astype(BF16)
    cc_ref[S5_ST:2 * S5_ST, :] = jnp.where(same, over_groups(-ct_im), 0.0).astype(BF16)


def _s5_prompt_kernel(u_ref, lr_ref, li_ref, ldt_ref, bdre_ref, bdim_ref, cdre_ref, cdim_ref, d_ref,
                      sg_ref, sgb_ref, hre_ref, him_ref,
                      bb_ref, cc_ref, ul_ref, sgl_ref, lhs_ref, bu_ref, hch_ref):
    a_re, a_im, f_re, f_im = _s5_discretize(lr_ref, li_ref, ldt_ref)

    @pl.when(pl.program_id(1) == 0)
    def _():
        _s5_fill_weights(f_re, f_im, bdre_ref, bdim_ref, cdre_ref, cdim_ref, bb_ref, cc_ref)

    are8 = jnp.broadcast_to(a_re, (S5_SEG, S5_ST))
    aim8 = jnp.broadcast_to(a_im, (S5_SEG, S5_ST))
    d = d_ref[0]
    for hh in range(S5_LH):
        ul_ref[hh] = u_ref[:, hh * V7X_LANES:(hh + 1) * V7X_LANES]

    def chunk_rows(c):
        return slice(c * S5_CR, (c + 1) * S5_CR)

    def project_in(c):
        for tl in range(S5_TC):
            t = c * S5_TC + tl
            for hh in range(S5_LH):
                lhs_ref[t * S5_SEG:(t + 1) * S5_SEG, hh * V7X_LANES:(hh + 1) * V7X_LANES] = (
                    ul_ref[hh, pl.ds(t, S5_SEG, stride=S5_SEGLEN), :])
        bu_ref[chunk_rows(c), :] = _dot(lhs_ref[chunk_rows(c), :].astype(BF16), bb_ref[...])

    def scan(c, hr, hi, keep):
        for tl in range(S5_TC):
            t = c * S5_TC + tl
            rows = slice(t * S5_SEG, (t + 1) * S5_SEG)
            br = bu_ref[rows, 0:S5_ST]
            bi = bu_ref[rows, S5_ST:2 * S5_ST]
            hr, hi = are8 * hr - aim8 * hi + br, are8 * hi + aim8 * hr + bi
            if keep:
                crow = slice(tl * S5_SEG, (tl + 1) * S5_SEG)
                hch_ref[c % 2, crow, 0:S5_ST] = hr
                hch_ref[c % 2, crow, S5_ST:2 * S5_ST] = hi
        return hr, hi

    def project_out(c):
        y = _dot(hch_ref[c % 2].astype(BF16), cc_ref[...]) + d * lhs_ref[chunk_rows(c), :]
        lhs_ref[chunk_rows(c), :] = jax.nn.gelu(y)
        for tl in range(S5_TC):
            t = c * S5_TC + tl
            for hh in range(S5_LH):
                sgl_ref[hh, pl.ds(t, S5_SEG, stride=S5_SEGLEN), :] = (
                    lhs_ref[t * S5_SEG:(t + 1) * S5_SEG, hh * V7X_LANES:(hh + 1) * V7X_LANES])

    zeros = jnp.zeros((S5_SEG, S5_ST), F32)
    e_re, e_im = zeros, zeros
    project_in(0)
    for c in range(S5_NC):
        if c + 1 < S5_NC:
            project_in(c + 1)
        e_re, e_im = scan(c, e_re, e_im, keep=False)

    p_re, p_im = a_re, a_im
    for _ in range(int(math.log2(S5_SEGLEN))):
        p_re, p_im = p_re * p_re - p_im * p_im, 2.0 * p_re * p_im
    seg = lax.broadcasted_iota(jnp.int32, (S5_SEG, S5_ST), 0)
    qr = qi = jnp.zeros((1, S5_ST), F32)
    h_re = h_im = zeros
    for s in range(1, S5_SEG):
        qr, qi = (p_re * qr - p_im * qi + e_re[s - 1:s, :],
                  p_re * qi + p_im * qr + e_im[s - 1:s, :])
        h_re = jnp.where(seg == s, qr, h_re)
        h_im = jnp.where(seg == s, qi, h_im)

    for c in range(S5_NC):
        h_re, h_im = scan(c, h_re, h_im, keep=True)
        if c > 0:
            project_out(c - 1)
    project_out(S5_NC - 1)

    hre_ref[0] = h_re[S5_SEG - 1:S5_SEG, :]
    him_ref[0] = h_im[S5_SEG - 1:S5_SEG, :]
    for hh in range(S5_LH):
        sg_ref[:, hh * V7X_LANES:(hh + 1) * V7X_LANES] = sgl_ref[hh]
        sgb_ref[:, hh * V7X_LANES:(hh + 1) * V7X_LANES] = sgl_ref[hh].astype(BF16)


def _s5_param_specs(idx):
    def vec(width):
        return pl.BlockSpec((1, 1, width), lambda *g: (idx(*g), 0, 0))

    def rows(shape):
        return pl.BlockSpec(shape, lambda *g: (idx(*g), 0))

    return [vec(S5_ST), vec(S5_ST), vec(S5_ST),
            rows((S5_ST, SSM_GROUP)), rows((S5_ST, SSM_GROUP)),
            rows((S5_CH, SSM_STATE)), rows((S5_CH, SSM_STATE)), vec(S5_CH)]


def s5_prompt(u, s5p):
    return pl.pallas_call(
        _s5_prompt_kernel,
        out_shape=(jax.ShapeDtypeStruct((P_ROWS, SSM_WIDTH), F32),
                   jax.ShapeDtypeStruct((P_ROWS, SSM_WIDTH), BF16),
                   jax.ShapeDtypeStruct((BATCH, 1, SSM_GROUPS * SSM_STATE), F32),
                   jax.ShapeDtypeStruct((BATCH, 1, SSM_GROUPS * SSM_STATE), F32)),
        grid=(S5_NB, BATCH),
        in_specs=[pl.BlockSpec((SEQ, S5_CH), lambda j, b: (b, j))] + _s5_param_specs(lambda j, b: j),
        out_specs=(pl.BlockSpec((SEQ, S5_CH), lambda j, b: (b, j)),
                   pl.BlockSpec((SEQ, S5_CH), lambda j, b: (b, j)),
                   pl.BlockSpec((1, 1, S5_ST), lambda j, b: (b, 0, j)),
                   pl.BlockSpec((1, 1, S5_ST), lambda j, b: (b, 0, j))),
        scratch_shapes=[
            pltpu.VMEM((S5_CH, 2 * S5_ST), BF16),
            pltpu.VMEM((2 * S5_ST, S5_CH), BF16),
            pltpu.VMEM((S5_LH, SEQ, V7X_LANES), F32),
            pltpu.VMEM((S5_LH, SEQ, V7X_LANES), F32),
            pltpu.VMEM((SEQ, S5_CH), F32),
            pltpu.VMEM((SEQ, 2 * S5_ST), F32),
            pltpu.VMEM((2, S5_CR, 2 * S5_ST), F32),
        ],
        compiler_params=_params(2),
        name="s5_prompt",
    )(u, *s5p)


def _s5_sample_kernel(u_ref, h0re_ref, h0im_ref, lr_ref, li_ref, ldt_ref, bdre_ref, bdim_ref,
                      cdre_ref, cdim_ref, d_ref, sg_ref, sgb_ref, hre_ref, him_ref, bb_ref, cc_ref):
    a_re, a_im, f_re, f_im = _s5_discretize(lr_ref, li_ref, ldt_ref)
    _s5_fill_weights(f_re, f_im, bdre_ref, bdim_ref, cdre_ref, cdim_ref, bb_ref, cc_ref)
    u = u_ref[...]
    bu = _dot(u.astype(BF16), bb_ref[...])
    h0r, h0i = h0re_ref[...], h0im_ref[...]
    hr = a_re * h0r - a_im * h0i + bu[:, 0:S5_ST]
    hi = a_re * h0i + a_im * h0r + bu[:, S5_ST:2 * S5_ST]
    hre_ref[...] = hr
    him_ref[...] = hi
    y = _dot(hr.astype(BF16), cc_ref[0:S5_ST, :]) + _dot(hi.astype(BF16), cc_ref[S5_ST:2 * S5_ST, :])
    sg = jax.nn.gelu(y + d_ref[0] * u)
    sg_ref[...] = sg
    sgb_ref[...] = sg.astype(BF16)


def s5_sample(u, h0_re, h0_im, s5p):
    st = pl.BlockSpec((S_ROWS, S5_ST), lambda j: (0, j))
    return pl.pallas_call(
        _s5_sample_kernel,
        out_shape=(jax.ShapeDtypeStruct((S_ROWS, SSM_WIDTH), F32),
                   jax.ShapeDtypeStruct((S_ROWS, SSM_WIDTH), BF16),
                   jax.ShapeDtypeStruct((S_ROWS, SSM_GROUPS * SSM_STATE), F32),
                   jax.ShapeDtypeStruct((S_ROWS, SSM_GROUPS * SSM_STATE), F32)),
        grid=(S5_NB,),
        in_specs=[pl.BlockSpec((S_ROWS, S5_CH), lambda j: (0, j)), st, st]
        + _s5_param_specs(lambda j: j),
        out_specs=(pl.BlockSpec((S_ROWS, S5_CH), lambda j: (0, j)),
                   pl.BlockSpec((S_ROWS, S5_CH), lambda j: (0, j)), st, st),
        scratch_shapes=[
            pltpu.VMEM((S5_CH, 2 * S5_ST), BF16),
            pltpu.VMEM((2 * S5_ST, S5_CH), BF16),
        ],
        compiler_params=_params(1),
        name="s5_sample",
    )(u, h0_re, h0_im, *s5p)


def s5_block_params(lam_re, lam_im, log_dt, b_re, b_im, c_re, c_im, d_skip):
    n_state = SSM_GROUPS * SSM_STATE
    return (lam_re.reshape(S5_NB, 1, S5_ST), lam_im.reshape(S5_NB, 1, S5_ST),
            jnp.repeat(log_dt, SSM_STATE).reshape(S5_NB, 1, S5_ST),
            b_re.reshape(n_state, SSM_GROUP), b_im.reshape(n_state, SSM_GROUP),
            c_re.reshape(SSM_WIDTH, SSM_STATE), c_im.reshape(SSM_WIDTH, SSM_STATE),
            d_skip.reshape(S5_NB, 1, S5_CH))


SSM_GLU_TN = 512


def _ssm_glu_kernel(ap_hbm, as_ref, w_ref, tp_ref, ts_ref, op_ref, os_ref, wc_ref, r_ref, ap_ref, sem):
    tile = _RowTile([ap_hbm], [ap_ref], [sem], SSM_WIDTH // SSM_GLU_TN)

    def body(col):
        wc_ref[...] = w_ref[...].astype(BF16)

        def mm(k):
            r_ref[_sub_rows(k), :] = _dot(ap_ref[_sub_rows(k), :], wc_ref[...])

        def epi(k):
            op_ref[_sub_rows(k), :] = (
                tp_ref[_sub_rows(k), :] * jax.nn.sigmoid(r_ref[_sub_rows(k), :])).astype(op_ref.dtype)

        _interleave(NSUB, tile.mm_hooks(col, mm), epi)

        @_on_last_row_tile
        def _():
            z = _dot(as_ref[...], wc_ref[...])
            os_ref[...] = (ts_ref[...] * jax.nn.sigmoid(z)).astype(os_ref.dtype)

    tile.run(body)


def ssm_glu(sg_p, sg_s, sgb_p, sgb_s, w_glu):
    tn = SSM_GLU_TN
    return pl.pallas_call(
        _ssm_glu_kernel,
        out_shape=(jax.ShapeDtypeStruct((P_ROWS, SSM_WIDTH), BF16),
                   jax.ShapeDtypeStruct((S_ROWS, SSM_WIDTH), BF16)),
        grid=(NPT, SSM_WIDTH // tn),
        in_specs=[
            _ANY,
            _resident((S_ROWS, SSM_WIDTH), lambda i, j: (0, 0)),
            pl.BlockSpec((SSM_WIDTH, tn), lambda i, j: (0, j)),
            pl.BlockSpec((TM, tn), lambda i, j: (i, j)),
            pl.BlockSpec((S_ROWS, tn), lambda i, j: (0, j)),
        ],
        out_specs=(pl.BlockSpec((TM, tn), lambda i, j: (i, j)),
                   pl.BlockSpec((S_ROWS, tn), _sample_cols)),
        scratch_shapes=[pltpu.VMEM((SSM_WIDTH, tn), BF16), pltpu.VMEM((TM, tn), F32)]
        + _row_tile_scratch([SSM_WIDTH]),
        compiler_params=_params(2),
        name="ssm_glu",
    )(sgb_p, sgb_s, w_glu, sg_p, sg_s)


MERGE_TN = 256


def _merge_kernel(cp_hbm, cs_ref, yp_hbm, ys_ref, wa_ref, wb_ref, gap_ref, gbp_ref, gas_ref, gbs_ref,
                  op_ref, os_ref, wc_ref, r_ref, cp_ref, yp_ref, csem, ysem):
    tile = _RowTile([cp_hbm, yp_hbm], [cp_ref, yp_ref], [csem, ysem], D_MODEL // MERGE_TN)

    def body(col):
        wc_ref[0] = wa_ref[...].astype(BF16)
        wc_ref[1] = wb_ref[...].astype(BF16)

        def mm(k):
            r_ref[0, _sub_rows(k), :] = _dot(cp_ref[_sub_rows(k), :], wc_ref[0])
            r_ref[1, _sub_rows(k), :] = _dot(yp_ref[_sub_rows(k), :], wc_ref[1])

        def epi(k):
            rows = _sub_rows(k)
            op_ref[rows, :] = (gap_ref[rows, :] * r_ref[0, rows, :]
                               + gbp_ref[rows, :] * r_ref[1, rows, :]).astype(op_ref.dtype)

        _interleave(NSUB, tile.mm_hooks(col, mm), epi)

        @_on_last_row_tile
        def _():
            ya = _dot(cs_ref[...], wc_ref[0])
            yb = _dot(ys_ref[...], wc_ref[1])
            os_ref[...] = (gas_ref[...] * ya + gbs_ref[...] * yb).astype(os_ref.dtype)

    tile.run(body)


def merge(c_p, c_s, yg_p, yg_s, w_conv_out, w_ssm_out, gates_p, gates_s):
    tn = MERGE_TN
    nj = D_MODEL // tn
    return pl.pallas_call(
        _merge_kernel,
        out_shape=(jax.ShapeDtypeStruct((P_ROWS, D_MODEL), BF16),
                   jax.ShapeDtypeStruct((S_ROWS, D_MODEL), BF16)),
        grid=(NPT, nj),
        in_specs=[
            _ANY,
            _resident((S_ROWS, CONV_WIDTH), lambda i, j: (0, 0)),
            _ANY,
            _resident((S_ROWS, SSM_WIDTH), lambda i, j: (0, 0)),
            pl.BlockSpec((CONV_WIDTH, tn), lambda i, j: (0, j)),
            pl.BlockSpec((SSM_WIDTH, tn), lambda i, j: (0, j)),
            pl.BlockSpec((TM, tn), lambda i, j: (i, j)),
            pl.BlockSpec((TM, tn), lambda i, j: (i, j + nj)),
            pl.BlockSpec((S_ROWS, tn), lambda i, j: (0, j)),
            pl.BlockSpec((S_ROWS, tn), lambda i, j: (0, j + nj)),
        ],
        out_specs=(pl.BlockSpec((TM, tn), lambda i, j: (i, j)),
                   pl.BlockSpec((S_ROWS, tn), _sample_cols)),
        scratch_shapes=[pltpu.VMEM((2, CONV_WIDTH, tn), BF16), pltpu.VMEM((2, TM, tn), F32)]
        + _row_tile_scratch([CONV_WIDTH, SSM_WIDTH]),
        compiler_params=_params(2),
        name="merge",
    )(c_p, c_s, yg_p, yg_s, w_conv_out, w_ssm_out, gates_p, gates_p, gates_s, gates_s)


OPROJ_TN = 512


def _oproj_kernel(ap_hbm, as_ref, w_ref, xp_ref, xs_ref, op_ref, os_ref, wc_ref, ap_ref, sem):
    tile = _RowTile([ap_hbm], [ap_ref], [sem], D_MODEL // OPROJ_TN)

    def body(col):
        wc_ref[...] = w_ref[...].astype(BF16)

        def mm(k):
            op_ref[_sub_rows(k), :] = _dot(ap_ref[_sub_rows(k), :], wc_ref[...])

        def epi(k):
            op_ref[_sub_rows(k), :] = op_ref[_sub_rows(k), :] + xp_ref[_sub_rows(k), :]

        _interleave(NSUB, tile.mm_hooks(col, mm), epi)

        @_on_last_row_tile
        def _():
            os_ref[...] = xs_ref[...] + _dot(as_ref[...], wc_ref[...])

    tile.run(body)


def oproj(merged_p, merged_s, w_o, xp, xs):
    tn = OPROJ_TN
    return pl.pallas_call(
        _oproj_kernel,
        out_shape=(jax.ShapeDtypeStruct((P_ROWS, D_MODEL), F32),
                   jax.ShapeDtypeStruct((S_ROWS, D_MODEL), F32)),
        grid=(NPT, D_MODEL // tn),
        in_specs=[
            _ANY,
            _resident((S_ROWS, D_MODEL), lambda i, j: (0, 0)),
            pl.BlockSpec((D_MODEL, tn), lambda i, j: (0, j)),
            pl.BlockSpec((TM, tn), lambda i, j: (i, j)),
            pl.BlockSpec((S_ROWS, tn), lambda i, j: (0, j)),
        ],
        out_specs=(pl.BlockSpec((TM, tn), lambda i, j: (i, j)),
                   pl.BlockSpec((S_ROWS, tn), _sample_cols)),
        scratch_shapes=[pltpu.VMEM((D_MODEL, tn), BF16)] + _row_tile_scratch([D_MODEL]),
        compiler_params=_params(2),
        name="oproj",
    )(merged_p, merged_s, w_o, xp, xs)


DOWN_SUBS = (128, 128, 128, 128)
DOWN_TM = sum(DOWN_SUBS)
DOWN_NPT = P_ROWS // DOWN_TM
DOWN_TN = 512
DOWN_NORM_ROWS = 64


def _ffn_down_kernel(ap_hbm, as_ref, w_ref, xp_ref, xs_ref, g_ref, yp_ref, os_ref, ap_ref, sem):
    tile = _RowTile([ap_hbm], [ap_ref], [sem], D_MODEL // DOWN_TN, sizes=DOWN_SUBS)
    cols = pl.ds(pl.multiple_of(pl.program_id(1) * DOWN_TN, DOWN_TN), DOWN_TN)

    def body(col):
        def mm(k):
            rows = tile.rows(k)
            yp_ref[rows, cols] = xp_ref[rows, :] + _dot(ap_ref[rows, :], w_ref[...])

        if col == "mid":
            yp_ref[:, cols] = xp_ref[...] + _dot(ap_ref[...], w_ref[...])
        else:
            for k in range(len(DOWN_SUBS)):
                tile.mm_hooks(col, mm)(k)

        @pl.when(pl.program_id(0) == DOWN_NPT - 1)
        def _():
            os_ref[...] = xs_ref[...] + _dot(as_ref[...], w_ref[...])

        if col == "last":
            def norm(c, carry):
                rows = pl.ds(pl.multiple_of(c * DOWN_NORM_ROWS, DOWN_NORM_ROWS), DOWN_NORM_ROWS)
                yp_ref[rows, :] = _rms(yp_ref[rows, :], g_ref[...])
                return carry

            lax.fori_loop(0, DOWN_TM // DOWN_NORM_ROWS, norm, 0)

    tile.run(body)


def ffn_down(h_p, h_s, w_down_bf16, x1_p, x1_s, final_g):
    tn = DOWN_TN
    last = DOWN_NPT - 1
    return pl.pallas_call(
        _ffn_down_kernel,
        out_shape=(jax.ShapeDtypeStruct((P_ROWS, D_MODEL), F32),
                   jax.ShapeDtypeStruct((S_ROWS, D_MODEL), F32)),
        grid=(DOWN_NPT, D_MODEL // tn),
        in_specs=[
            _ANY,
            _resident((S_ROWS, D_FF), lambda i, j: (0, 0)),
            pl.BlockSpec((D_FF, tn), lambda i, j: (0, j)),
            pl.BlockSpec((DOWN_TM, tn), lambda i, j: (i, j)),
            pl.BlockSpec((S_ROWS, tn), lambda i, j: (0, j)),
            pl.BlockSpec((1, D_MODEL), lambda i, j: (0, 0)),
        ],
        out_specs=(pl.BlockSpec((DOWN_TM, D_MODEL), lambda i, j: (i, 0)),
                   pl.BlockSpec((S_ROWS, tn), lambda i, j: (0, jnp.where(i == last, j, 0)))),
        scratch_shapes=_row_tile_scratch([D_FF], sizes=DOWN_SUBS),
        compiler_params=_params(2),
        name="ffn_down",
    )(h_p, h_s, w_down_bf16, x1_p, x1_s, final_g)


FFN_TN = 256
FFN_NJ = D_FF // FFN_TN
FFN_PAD = V7X_SUBLANES


def _ffn_up_kernel(ap_hbm, as_ref, wg_ref, wv_ref, st_ref, cw_ref, cb_ref, wd_ref,
                   hp_ref, hs_ref, tail_ref, nst_ref, wdb_ref, wc_ref, g_ref, v_ref, ap_ref, sem):
    tn = FFN_TN
    tile = _RowTile([ap_hbm], [ap_ref], [sem], FFN_NJ)

    def body(col):
        wc_ref[:, 0:tn] = wg_ref[...].astype(BF16)
        wc_ref[:, tn:2 * tn] = wv_ref[...].astype(BF16)
        wdb_ref[...] = wd_ref[...].astype(BF16)
        cols = pl.ds(pl.multiple_of(pl.program_id(1) * tn, tn), tn)
        w0, w1, w2 = cw_ref[0:1, cols], cw_ref[1:2, cols], cw_ref[2:3, cols]
        cb = cb_ref[:, cols]
        g_ref[0:FFN_PAD, :] = jnp.zeros((FFN_PAD, tn), F32)

        def mm(k):
            r = _dot(ap_ref[_sub_rows(k), :], wc_ref[...])
            g_ref[_sub_rows(k, FFN_PAD), :] = r[:, 0:tn]
            v_ref[_sub_rows(k), :] = r[:, tn:2 * tn]

        def epi(k):
            gc = (w0 * g_ref[_sub_rows(k, FFN_PAD - 2), :] + w1 * g_ref[_sub_rows(k, FFN_PAD - 1), :]
                  + w2 * g_ref[_sub_rows(k, FFN_PAD), :] + cb)
            hp_ref[_sub_rows(k), :] = (gc * jax.nn.sigmoid(gc) * v_ref[_sub_rows(k), :]).astype(hp_ref.dtype)

        _interleave(NSUB, tile.mm_hooks(col, mm), epi)
        tail_ref[0, :, cols] = g_ref[TM:TM + FFN_PAD, :]

        @_on_last_row_tile
        def _():
            r = _dot(as_ref[...], wc_ref[...])
            gate, val = r[:, 0:tn], r[:, tn:2 * tn]
            ln = V7X_LANES
            old = [st_ref[:, (2 * c) * ln:(2 * c + 1) * ln] for c in range(tn // ln)]
            new = [st_ref[:, (2 * c + 1) * ln:(2 * c + 2) * ln] for c in range(tn // ln)]
            gc = w0 * jnp.concatenate(old, axis=1) + w1 * jnp.concatenate(new, axis=1) + w2 * gate + cb
            hs_ref[...] = (gc * jax.nn.sigmoid(gc) * val).astype(hs_ref.dtype)
            for c in range(tn // ln):
                nst_ref[:, (2 * c) * ln:(2 * c + 1) * ln] = new[c]
                nst_ref[:, (2 * c + 1) * ln:(2 * c + 2) * ln] = gate[:, c * ln:(c + 1) * ln]

    tile.run(body)


def ffn_up(xn2_p, xn2_s, w_up, ffn_state, ffn_conv_w, ffn_conv_b, w_down):
    nj = FFN_NJ
    tn = FFN_TN
    wd_rows = D_FF // (NPT * nj)

    def wd_slab(i, j):
        return (i * nj + j, 0)

    whole_row = lambda i, j: (0, 0)

    return pl.pallas_call(
        _ffn_up_kernel,
        out_shape=(jax.ShapeDtypeStruct((P_ROWS, D_FF), BF16),
                   jax.ShapeDtypeStruct((S_ROWS, D_FF), BF16),
                   jax.ShapeDtypeStruct((NPT, FFN_PAD, D_FF), F32),
                   jax.ShapeDtypeStruct((S_ROWS, 2 * D_FF), F32),
                   jax.ShapeDtypeStruct((D_FF, D_MODEL), BF16)),
        grid=(NPT, nj),
        in_specs=[
            _ANY,
            _resident((S_ROWS, D_MODEL), lambda i, j: (0, 0)),
            pl.BlockSpec((D_MODEL, tn), lambda i, j: (0, j)),
            pl.BlockSpec((D_MODEL, tn), lambda i, j: (0, j + nj)),
            pl.BlockSpec((S_ROWS, 2 * tn), _sample_cols),
            pl.BlockSpec((FFN_K, D_FF), whole_row),
            pl.BlockSpec((1, D_FF), whole_row),
            pl.BlockSpec((wd_rows, D_MODEL), wd_slab),
        ],
        out_specs=(
            pl.BlockSpec((TM, tn), lambda i, j: (i, j)),
            pl.BlockSpec((S_ROWS, tn), _sample_cols),
            pl.BlockSpec((1, FFN_PAD, D_FF), lambda i, j: (i, 0, 0)),
            pl.BlockSpec((S_ROWS, 2 * tn), _sample_cols),
            pl.BlockSpec((wd_rows, D_MODEL), wd_slab),
        ),
        scratch_shapes=[
            pltpu.VMEM((D_MODEL, 2 * tn), BF16),
            pltpu.VMEM((FFN_PAD + TM, tn), F32),
            pltpu.VMEM((TM, tn), F32),
        ] + _row_tile_scratch([D_MODEL]),
        compiler_params=_params(2),
        name="ffn_up",
    )(xn2_p, xn2_s, w_up, w_up, ffn_state, ffn_conv_w, ffn_conv_b, w_down)


def kernel(x_prompt, x_sample, state_conv, state_ssm_re, state_ssm_im, state_ffn_conv,
           norm_mix_g, w_in, conv_w, conv_b, ln_g, ln_b, w_conv_out,
           lam_re, lam_im, log_dt, b_re, b_im, c_re, c_im, d_skip, w_glu, w_ssm_out, w_o,
           norm_ffn_g, w_up, ffn_conv_w, ffn_conv_b, w_down, final_norm_g):
    xp = x_prompt.reshape(P_ROWS, D_MODEL)
    xs = x_sample.reshape(S_ROWS, D_MODEL)

    def row(v):
        return v.reshape(1, -1)

    xn_p, xn_s = rownorm_pair(xp, xs, row(norm_mix_g[0]), BF16)
    glu_p, glu_s = inproj_glu(xn_p, xn_s, w_in[0])
    u_p, u_s = inproj_cols(xn_p, xn_s, w_in[0], 2 * CONV_WIDTH, SSM_WIDTH, False, "inproj_ssm")
    gates_p, gates_s = inproj_cols(xn_p, xn_s, w_in[0], 2 * CONV_WIDTH + SSM_WIDTH, 2 * D_MODEL, True,
                                   "inproj_gates")

    conv_vecs = (conv_w[0], row(conv_b[0]), row(ln_g[0]), row(ln_b[0]))
    c_p, conv_p = conv_prompt(glu_p, *conv_vecs)
    c_s, conv_s = conv_sample(glu_s, state_conv[0].transpose(1, 0, 2), *conv_vecs)
    conv_s = conv_s.transpose(1, 0, 2)

    s5p = s5_block_params(lam_re[0], lam_im[0], log_dt[0], b_re[0], b_im[0], c_re[0], c_im[0], d_skip[0])
    n_state = SSM_GROUPS * SSM_STATE
    sg_p, sgb_p, ssr_p, ssi_p = s5_prompt(u_p, s5p)
    sg_s, sgb_s, ssr_s, ssi_s = s5_sample(u_s, state_ssm_re[0].reshape(S_ROWS, n_state),
                                          state_ssm_im[0].reshape(S_ROWS, n_state), s5p)
    yg_p, yg_s = ssm_glu(sg_p, sg_s, sgb_p, sgb_s, w_glu[0])

    merged_p, merged_s = merge(c_p, c_s, yg_p, yg_s, w_conv_out[0], w_ssm_out[0], gates_p, gates_s)
    x1_p, x1_s = oproj(merged_p, merged_s, w_o[0], xp, xs)

    xn2_p, xn2_s = rownorm_pair(x1_p, x1_s, row(norm_ffn_g[0]), BF16)
    n_ct = D_FF // V7X_LANES
    ffn_state = (state_ffn_conv[0].reshape(S_ROWS, FFN_K - 1, n_ct, V7X_LANES)
                 .transpose(0, 2, 1, 3).reshape(S_ROWS, (FFN_K - 1) * D_FF))
    h_p, h_s, gate_tail, ffn_state_new, w_down_bf16 = ffn_up(
        xn2_p, xn2_s, w_up[0], ffn_state, ffn_conv_w[0], row(ffn_conv_b[0]), w_down[0])
    y_p, x2_s = ffn_down(h_p, h_s, w_down_bf16, x1_p, x1_s, row(final_norm_g))
    y_s = rownorm(x2_s, row(final_norm_g), F32, S_ROWS)

    ffn_p = gate_tail[:, FFN_PAD - (FFN_K - 1):, :]
    ffn_s = (ffn_state_new.reshape(S_ROWS, n_ct, FFN_K - 1, V7X_LANES)
             .transpose(0, 2, 1, 3).reshape(S_ROWS, FFN_K - 1, D_FF))
    state_shape = (1, -1, SSM_GROUPS, SSM_STATE)
    return (y_p.reshape(BATCH, SEQ, D_MODEL), y_s.reshape(DEC_BATCH, 1, D_MODEL),
            conv_p[None], conv_s[None],
            ssr_p.reshape(state_shape), ssi_p.reshape(state_shape),
            ssr_s.reshape(state_shape), ssi_s.reshape(state_shape),
            ffn_p[None], ffn_s[None])
```

```python
import functools
import math

import jax
import jax.numpy as jnp
from jax import lax
from jax.experimental import pallas as pl
from jax.experimental.pallas import tpu as pltpu

D_MODEL = 4096
BATCH = 4
SEQ = 2048
DEC_BATCH = 128
CONV_WIDTH = D_MODEL // 2
CONV_K = 31
SSM_WIDTH = D_MODEL // 2
SSM_GROUP = 16
SSM_GROUPS = SSM_WIDTH // SSM_GROUP
SSM_STATE = 64
D_FF = 11008
FFN_K = 3
EPS = 1e-6

P_ROWS = BATCH * SEQ
S_ROWS = DEC_BATCH

V7X_SUBLANES = 8
V7X_LANES = 128
V7X_VMEM_LIMIT = 58 * 1024 * 1024

TM = SEQ
NPT = P_ROWS // TM
SUB_SIZES = (768, 768, 384, 128)
SUB_STARTS = tuple(sum(SUB_SIZES[:k]) for k in range(len(SUB_SIZES)))
NSUB = len(SUB_SIZES)
assert sum(SUB_SIZES) == TM

BF16 = jnp.bfloat16
F32 = jnp.float32


def _params(n_grid_dims):
    return pltpu.CompilerParams(
        dimension_semantics=("arbitrary",) * n_grid_dims,
        vmem_limit_bytes=V7X_VMEM_LIMIT)


def _dot(a, b):
    return jnp.dot(a, b, preferred_element_type=F32)


def _rms(x, g):
    return x * lax.rsqrt(jnp.mean(x * x, axis=-1, keepdims=True) + EPS) * g


def _interleave(n, mm, epi):
    mm(0)
    for k in range(1, n):
        mm(k)
        epi(k - 1)
    epi(n - 1)


def _sub_rows(k, offset=0):
    return slice(offset + SUB_STARTS[k], offset + SUB_STARTS[k] + SUB_SIZES[k])


class _RowTile:
    def __init__(self, hbm_refs, bufs, sems, ncol, sizes=SUB_SIZES):
        self.hbm_refs, self.bufs, self.sems, self.ncol = hbm_refs, bufs, sems, ncol
        self.sizes = sizes
        self.starts = tuple(sum(sizes[:k]) for k in range(len(sizes)))
        self.tm = sum(sizes)
        self.i, self.j = pl.program_id(0), pl.program_id(1)

    def rows(self, k):
        return slice(self.starts[k], self.starts[k] + self.sizes[k])

    def _copies(self, tile, k):
        return [pltpu.make_async_copy(hbm.at[pl.ds(tile * self.tm + self.starts[k], self.sizes[k]), :],
                                      buf.at[pl.ds(self.starts[k], self.sizes[k]), :], sem.at[k])
                for hbm, buf, sem in zip(self.hbm_refs, self.bufs, self.sems)]

    def fetch_first_tile(self):
        @pl.when((self.i == 0) & (self.j == 0))
        def _():
            for k in range(len(self.sizes)):
                for c in self._copies(0, k):
                    c.start()

    def wait(self, k):
        for c in self._copies(self.i, k):
            c.wait()

    def fetch_next(self, k):
        @pl.when(self.i < P_ROWS // self.tm - 1)
        def _():
            for c in self._copies(self.i + 1, k):
                c.start()

    def run(self, body):
        assert self.ncol >= 2
        self.fetch_first_tile()
        pl.when(self.j == 0)(functools.partial(body, "first"))
        pl.when((self.j > 0) & (self.j < self.ncol - 1))(functools.partial(body, "mid"))
        pl.when(self.j == self.ncol - 1)(functools.partial(body, "last"))

    def mm_hooks(self, col, mm):
        def wrapped(k):
            if col == "first":
                self.wait(k)
            mm(k)
            if col == "last":
                self.fetch_next(k)
        return wrapped


_ANY = pl.BlockSpec(memory_space=pl.ANY)


def _row_tile_scratch(widths, sizes=SUB_SIZES):
    return ([pltpu.VMEM((sum(sizes), w), BF16) for w in widths]
            + [pltpu.SemaphoreType.DMA((len(sizes),)) for _ in widths])


def _resident(shape, index_map):
    return pl.BlockSpec(shape, index_map, pipeline_mode=pl.Buffered(1))


def _sample_cols(i, j):
    return (0, jnp.where(i == NPT - 1, j, 0))


def _on_last_row_tile(fn):
    pl.when(pl.program_id(0) == NPT - 1)(fn)


def _rownorm_kernel(x_ref, g_ref, o_ref):
    o_ref[...] = _rms(x_ref[...], g_ref[...]).astype(o_ref.dtype)


def rownorm(x, g, out_dtype, row_block):
    rows = x.shape[0]
    return pl.pallas_call(
        _rownorm_kernel,
        out_shape=jax.ShapeDtypeStruct((rows, D_MODEL), out_dtype),
        grid=(rows // row_block,),
        in_specs=[
            pl.BlockSpec((row_block, D_MODEL), lambda i: (i, 0)),
            pl.BlockSpec((1, D_MODEL), lambda i: (0, 0)),
        ],
        out_specs=pl.BlockSpec((row_block, D_MODEL), lambda i: (i, 0)),
        compiler_params=_params(1),
        name="rownorm",
    )(x, g)


def rownorm_pair(xp, xs, g, out_dtype):
    return rownorm(xp, g, out_dtype, 512), rownorm(xs, g, out_dtype, S_ROWS)


GLU_TN = 256


def _inproj_glu_kernel(ap_hbm, as_ref, wa_ref, wg_ref, op_ref, os_ref, wc_ref, r_ref, ap_ref, sem):
    tn = GLU_TN
    tile = _RowTile([ap_hbm], [ap_ref], [sem], CONV_WIDTH // tn)

    def body(col):
        wc_ref[:, 0:tn] = wa_ref[...].astype(BF16)
        wc_ref[:, tn:2 * tn] = wg_ref[...].astype(BF16)

        def mm(k):
            r_ref[_sub_rows(k), :] = _dot(ap_ref[_sub_rows(k), :], wc_ref[...])

        def epi(k):
            r = r_ref[_sub_rows(k), :]
            op_ref[_sub_rows(k), :] = r[:, 0:tn] * jax.nn.sigmoid(r[:, tn:2 * tn])

        _interleave(NSUB, tile.mm_hooks(col, mm), epi)

        @_on_last_row_tile
        def _():
            r = _dot(as_ref[...], wc_ref[...])
            os_ref[...] = r[:, 0:tn] * jax.nn.sigmoid(r[:, tn:2 * tn])

    tile.run(body)


def inproj_glu(xn_p, xn_s, w_in):
    tn = GLU_TN
    nj = CONV_WIDTH // tn
    return pl.pallas_call(
        _inproj_glu_kernel,
        out_shape=(jax.ShapeDtypeStruct((P_ROWS, CONV_WIDTH), F32),
                   jax.ShapeDtypeStruct((S_ROWS, CONV_WIDTH), F32)),
        grid=(NPT, nj),
        in_specs=[
            _ANY,
            _resident((S_ROWS, D_MODEL), lambda i, j: (0, 0)),
            pl.BlockSpec((D_MODEL, tn), lambda i, j: (0, j)),
            pl.BlockSpec((D_MODEL, tn), lambda i, j: (0, j + nj)),
        ],
        out_specs=(pl.BlockSpec((TM, tn), lambda i, j: (i, j)),
                   pl.BlockSpec((S_ROWS, tn), _sample_cols)),
        scratch_shapes=[pltpu.VMEM((D_MODEL, 2 * tn), BF16), pltpu.VMEM((TM, 2 * tn), F32)]
        + _row_tile_scratch([D_MODEL]),
        compiler_params=_params(2),
        name="inproj_glu",
    )(xn_p, xn_s, w_in, w_in)


def _inproj_cols_kernel(ap_hbm, as_ref, w_ref, op_ref, os_ref, wc_ref, ap_ref, sem, *, squash, ncol):
    tile = _RowTile([ap_hbm], [ap_ref], [sem], ncol)

    def body(col):
        wc_ref[...] = w_ref[...].astype(BF16)

        def mm(k):
            op_ref[_sub_rows(k), :] = _dot(ap_ref[_sub_rows(k), :], wc_ref[...])

        def epi(k):
            if squash:
                op_ref[_sub_rows(k), :] = jax.nn.sigmoid(op_ref[_sub_rows(k), :])

        _interleave(NSUB, tile.mm_hooks(col, mm), epi)

        @_on_last_row_tile
        def _():
            r = _dot(as_ref[...], wc_ref[...])
            os_ref[...] = jax.nn.sigmoid(r) if squash else r

    tile.run(body)


def inproj_cols(xn_p, xn_s, w_in, col0, width, squash, name):
    tn = 512
    return pl.pallas_call(
        functools.partial(_inproj_cols_kernel, squash=squash, ncol=width // tn),
        out_shape=(jax.ShapeDtypeStruct((P_ROWS, width), F32),
                   jax.ShapeDtypeStruct((S_ROWS, width), F32)),
        grid=(NPT, width // tn),
        in_specs=[
            _ANY,
            _resident((S_ROWS, D_MODEL), lambda i, j: (0, 0)),
            pl.BlockSpec((D_MODEL, tn), lambda i, j: (0, j + col0 // tn)),
        ],
        out_specs=(pl.BlockSpec((TM, tn), lambda i, j: (i, j)),
                   pl.BlockSpec((S_ROWS, tn), _sample_cols)),
        scratch_shapes=[pltpu.VMEM((D_MODEL, tn), BF16)] + _row_tile_scratch([D_MODEL]),
        compiler_params=_params(2),
        name=name,
    )(xn_p, xn_s, w_in)


CONV_TT = 256
CONV_HALO = 32
CONV_RC = 64
CONV_LC = 256
CONV_PIECE = 32


def _ln_swish(c, g, b):
    mu = jnp.mean(c, axis=-1, keepdims=True)
    d = c - mu
    var = jnp.mean(d * d, axis=-1, keepdims=True)
    r = d * lax.rsqrt(var + EPS) * g + b
    return r * jax.nn.sigmoid(r)


def _conv_prompt_kernel(x_ref, w_ref, cb_ref, g_ref, b_ref, o_ref, st_ref, xs_ref, cbuf_ref, w8_ref):
    t = pl.program_id(1)
    nt = pl.num_programs(1)
    sub = V7X_SUBLANES

    @pl.when(t == 0)
    def _():
        xs_ref[0, 0:CONV_HALO, :] = jnp.zeros((CONV_HALO, CONV_WIDTH), F32)
        xs_ref[0, CONV_HALO + CONV_TT:CONV_HALO + CONV_TT + sub, :] = jnp.zeros((sub, CONV_WIDTH), F32)
        for k in range(CONV_K):
            w8_ref[k] = jnp.broadcast_to(w_ref[k:k + 1, :], (sub, CONV_WIDTH))

    xs_ref[0, CONV_HALO:CONV_HALO + CONV_TT, :] = x_ref[...]

    def shift_piece(p, carry):
        r = pl.multiple_of(p * CONV_PIECE, CONV_PIECE)
        piece = xs_ref[0, pl.ds(r, CONV_PIECE + sub), :]
        for m in range(1, sub):
            rolled = pltpu.roll(piece, CONV_PIECE + sub - m, axis=0)
            xs_ref[m, pl.ds(r, CONV_PIECE), :] = rolled[0:CONV_PIECE]
        return carry

    lax.fori_loop(0, (CONV_HALO + CONV_TT) // CONV_PIECE, shift_piece, 0)

    off = CONV_HALO - (CONV_K - 1)

    lane_chunks = [slice(l0, l0 + CONV_LC) for l0 in range(0, CONV_WIDTH, CONV_LC)]

    def chunk(c, carry):
        r0 = pl.multiple_of(c * CONV_RC, CONV_RC)
        part = jnp.zeros((CONV_RC, CONV_LC), F32)
        for lanes in lane_chunks:
            accs = [jnp.zeros((sub, CONV_LC), F32) for _ in range(CONV_RC // sub)]
            for k in range(CONV_K):
                o = off + k
                wk = w8_ref[k, :, lanes]
                for rg in range(CONV_RC // sub):
                    win = xs_ref[o % sub, pl.ds(r0 + (o // sub + rg) * sub, sub), lanes]
                    accs[rg] = accs[rg] + win * wk
            acc = jnp.concatenate(accs, axis=0) + cb_ref[:, lanes]
            cbuf_ref[:, lanes] = acc
            part = part + acc
        mu = jnp.sum(part, axis=-1, keepdims=True) / CONV_WIDTH
        part = jnp.zeros((CONV_RC, CONV_LC), F32)
        for lanes in lane_chunks:
            d = cbuf_ref[:, lanes] - mu
            part = part + d * d
        rstd = lax.rsqrt(jnp.sum(part, axis=-1, keepdims=True) / CONV_WIDTH + EPS)
        for lanes in lane_chunks:
            r = (cbuf_ref[:, lanes] - mu) * rstd * g_ref[:, lanes] + b_ref[:, lanes]
            o_ref[pl.ds(r0, CONV_RC), lanes] = (r * jax.nn.sigmoid(r)).astype(o_ref.dtype)
        return carry

    lax.fori_loop(0, CONV_TT // CONV_RC, chunk, 0)

    @pl.when(t == nt - 1)
    def _():
        st_ref[0] = xs_ref[off, CONV_TT:CONV_TT + CONV_K - 1, :]

    xs_ref[0, 0:CONV_HALO, :] = xs_ref[0, CONV_TT:CONV_TT + CONV_HALO, :]


def conv_prompt(glu, conv_w, conv_b, ln_g, ln_b):
    nt = SEQ // CONV_TT
    vec = pl.BlockSpec((1, CONV_WIDTH), lambda b, t: (0, 0))
    return pl.pallas_call(
        _conv_prompt_kernel,
        out_shape=(jax.ShapeDtypeStruct((P_ROWS, CONV_WIDTH), BF16),
                   jax.ShapeDtypeStruct((BATCH, CONV_K - 1, CONV_WIDTH), F32)),
        grid=(BATCH, nt),
        in_specs=[
            pl.BlockSpec((CONV_TT, CONV_WIDTH), lambda b, t: (b * nt + t, 0)),
            pl.BlockSpec((CONV_K, CONV_WIDTH), lambda b, t: (0, 0)),
            vec, vec, vec,
        ],
        out_specs=(pl.BlockSpec((CONV_TT, CONV_WIDTH), lambda b, t: (b * nt + t, 0)),
                   pl.BlockSpec((1, CONV_K - 1, CONV_WIDTH), lambda b, t: (b, 0, 0))),
        scratch_shapes=[
            pltpu.VMEM((V7X_SUBLANES, CONV_HALO + CONV_TT + V7X_SUBLANES, CONV_WIDTH), F32),
            pltpu.VMEM((CONV_RC, CONV_WIDTH), F32),
            pltpu.VMEM((CONV_K, V7X_SUBLANES, CONV_WIDTH), F32),
        ],
        compiler_params=_params(2),
        name="conv_prompt",
    )(glu, conv_w, conv_b, ln_g, ln_b)


CONV_SB = 16


def _conv_sample_kernel(x_ref, st_ref, w_ref, cb_ref, g_ref, b_ref, o_ref, nst_ref):
    nb = CONV_K - 1
    x = x_ref[...]
    acc = x * w_ref[nb:CONV_K, :] + cb_ref[...]
    for k in range(nb):
        acc = acc + st_ref[k] * w_ref[k:k + 1, :]
    o_ref[...] = _ln_swish(acc, g_ref[...], b_ref[...]).astype(o_ref.dtype)
    for k in range(nb - 1):
        nst_ref[k] = st_ref[k + 1]
    nst_ref[nb - 1] = x


def conv_sample(glu, state_tsc, conv_w, conv_b, ln_g, ln_b):
    vec = pl.BlockSpec((1, CONV_WIDTH), lambda i: (0, 0))
    st_spec = pl.BlockSpec((CONV_K - 1, CONV_SB, CONV_WIDTH), lambda i: (0, i, 0))
    return pl.pallas_call(
        _conv_sample_kernel,
        out_shape=(jax.ShapeDtypeStruct((S_ROWS, CONV_WIDTH), BF16),
                   jax.ShapeDtypeStruct((CONV_K - 1, S_ROWS, CONV_WIDTH), F32)),
        grid=(S_ROWS // CONV_SB,),
        in_specs=[
            pl.BlockSpec((CONV_SB, CONV_WIDTH), lambda i: (i, 0)),
            st_spec,
            pl.BlockSpec((CONV_K, CONV_WIDTH), lambda i: (0, 0)),
            vec, vec, vec,
        ],
        out_specs=(pl.BlockSpec((CONV_SB, CONV_WIDTH), lambda i: (i, 0)), st_spec),
        compiler_params=_params(1),
        name="conv_sample",
    )(glu, state_tsc, conv_w, conv_b, ln_g, ln_b)


S5_GB = 16
S5_CH = S5_GB * SSM_GROUP
S5_ST = S5_GB * SSM_STATE
S5_NB = SSM_GROUPS // S5_GB
S5_SEG = V7X_SUBLANES
S5_SEGLEN = SEQ // S5_SEG
S5_TC = 64
S5_NC = S5_SEGLEN // S5_TC
S5_CR = S5_TC * S5_SEG
S5_LH = S5_CH // V7X_LANES


def _s5_discretize(lr_ref, li_ref, ldt_ref):
    lr, li = lr_ref[0], li_ref[0]
    dt = jnp.exp(ldt_ref[0])
    mag = jnp.exp(lr * dt)
    a_re, a_im = mag * jnp.cos(li * dt), mag * jnp.sin(li * dt)
    den = lr * lr + li * li
    nr, ni = a_re - 1.0, a_im
    f_re = (nr * lr + ni * li) / den
    f_im = (ni * lr - nr * li) / den
    return a_re, a_im, f_re, f_im


def _exact_transpose(x):
    k = x.shape[1]
    eye = jnp.where(lax.broadcasted_iota(jnp.int32, (k, k), 0) == lax.broadcasted_iota(jnp.int32, (k, k), 1),
                    1.0, 0.0).astype(BF16)

    def through_identity(piece):
        return lax.dot_general(eye, piece, (((1,), (1,)), ((), ())), preferred_element_type=F32)

    hi = x.astype(BF16)
    rest = x - hi.astype(F32)
    mid = rest.astype(BF16)
    lo = (rest - mid.astype(F32)).astype(BF16)
    return through_identity(hi) + through_identity(mid) + through_identity(lo)


def _s5_fill_weights(f_re, f_im, bre_ref, bim_ref, cre_ref, cim_ref, bb_ref, cc_ref):
    def over_groups(v):
        return jnp.concatenate([v] * S5_GB, axis=0)

    bt_re, bt_im = _exact_transpose(bre_ref[...]), _exact_transpose(bim_ref[...])
    same = (lax.broadcasted_iota(jnp.int32, (S5_CH, S5_ST), 0) // SSM_GROUP
            == lax.broadcasted_iota(jnp.int32, (S5_CH, S5_ST), 1) // SSM_STATE)
    bb_ref[:, 0:S5_ST] = jnp.where(same, over_groups(f_re * bt_re - f_im * bt_im), 0.0).astype(BF16)
    bb_ref[:, S5_ST:2 * S5_ST] = jnp.where(same, over_groups(f_re * bt_im + f_im * bt_re), 0.0).astype(BF16)

    ct_re, ct_im = _exact_transpose(cre_ref[...]), _exact_transpose(cim_ref[...])
    same = (lax.broadcasted_iota(jnp.int32, (S5_ST, S5_CH), 0) // SSM_STATE
            == lax.broadcasted_iota(jnp.int32, (S5_ST, S5_CH), 1) // SSM_GROUP)
    cc_ref[0:S5_ST, :] = jnp.where(same, over_groups(ct_re), 0.0).astype(BF16)
    cc_ref[S5_ST:2 * S5_ST, :] = jnp.where(same, over_groups(-ct_im), 0.0).astype(BF16)


def _s5_prompt_kernel(u_ref, lr_ref, li_ref, ldt_ref, bdre_ref, bdim_ref, cdre_ref, cdim_ref, d_ref,
                      sg_ref, sgb_ref, hre_ref, him_ref,
                      bb_ref, cc_ref, ul_ref, sgl_ref, lhs_ref, bu_ref, hch_ref):
    a_re, a_im, f_re, f_im = _s5_discretize(lr_ref, li_ref, ldt_ref)

    @pl.when(pl.program_id(1) == 0)
    def _():
        _s5_fill_weights(f_re, f_im, bdre_ref, bdim_ref, cdre_ref, cdim_ref, bb_ref, cc_ref)

    are8 = jnp.broadcast_to(a_re, (S5_SEG, S5_ST))
    aim8 = jnp.broadcast_to(a_im, (S5_SEG, S5_ST))
    d = d_ref[0]
    for hh in range(S5_LH):
        ul_ref[hh] = u_ref[:, hh * V7X_LANES:(hh + 1) * V7X_LANES]

    def chunk_rows(c):
        return slice(c * S5_CR, (c + 1) * S5_CR)

    def project_in(c):
        for tl in range(S5_TC):
            t = c * S5_TC + tl
            for hh in range(S5_LH):
                lhs_ref[t * S5_SEG:(t + 1) * S5_SEG, hh * V7X_LANES:(hh + 1) * V7X_LANES] = (
                    ul_ref[hh, pl.ds(t, S5_SEG, stride=S5_SEGLEN), :])
        bu_ref[chunk_rows(c), :] = _dot(lhs_ref[chunk_rows(c), :].astype(BF16), bb_ref[...])

    def scan(c, hr, hi, keep):
        for tl in range(S5_TC):
            t = c * S5_TC + tl
            rows = slice(t * S5_SEG, (t + 1) * S5_SEG)
            br = bu_ref[rows, 0:S5_ST]
            bi = bu_ref[rows, S5_ST:2 * S5_ST]
            hr, hi = are8 * hr - aim8 * hi + br, are8 * hi + aim8 * hr + bi
            if keep:
                crow = slice(tl * S5_SEG, (tl + 1) * S5_SEG)
                hch_ref[c % 2, crow, 0:S5_ST] = hr
                hch_ref[c % 2, crow, S5_ST:2 * S5_ST] = hi
        return hr, hi

    def project_out(c):
        y = _dot(hch_ref[c % 2].astype(BF16), cc_ref[...]) + d * lhs_ref[chunk_rows(c), :]
        lhs_ref[chunk_rows(c), :] = jax.nn.gelu(y)
        for tl in range(S5_TC):
            t = c * S5_TC + tl
            for hh in range(S5_LH):
                sgl_ref[hh, pl.ds(t, S5_SEG, stride=S5_SEGLEN), :] = (
                    lhs_ref[t * S5_SEG:(t + 1) * S5_SEG, hh * V7X_LANES:(hh + 1) * V7X_LANES])

    zeros = jnp.zeros((S5_SEG, S5_ST), F32)
    e_re, e_im = zeros, zeros
    project_in(0)
    for c in range(S5_NC):
        if c + 1 < S5_NC:
            project_in(c + 1)
        e_re, e_im = scan(c, e_re, e_im, keep=False)

    p_re, p_im = a_re, a_im
    for _ in range(int(math.log2(S5_SEGLEN))):
        p_re, p_im = p_re * p_re - p_im * p_im, 2.0 * p_re * p_im
    seg = lax.broadcasted_iota(jnp.int32, (S5_SEG, S5_ST), 0)
    qr = qi = jnp.zeros((1, S5_ST), F32)
    h_re = h_im = zeros
    for s in range(1, S5_SEG):
        qr, qi = (p_re * qr - p_im * qi + e_re[s - 1:s, :],
                  p_re * qi + p_im * qr + e_im[s - 1:s, :])
        h_re = jnp.where(seg == s, qr, h_re)
        h_im = jnp.where(seg == s, qi, h_im)

    for c in range(S5_NC):
        h_re, h_im = scan(c, h_re, h_im, keep=True)
        if c > 0:
            project_out(c - 1)
    project_out(S5_NC - 1)

    hre_ref[0] = h_re[S5_SEG - 1:S5_SEG, :]
    him_ref[0] = h_im[S5_SEG - 1:S5_SEG, :]
    for hh in range(S5_LH):
        sg_ref[:, hh * V7X_LANES:(hh + 1) * V7X_LANES] = sgl_ref[hh]
        sgb_ref[:, hh * V7X_LANES:(hh + 1) * V7X_LANES] = sgl_ref[hh].astype(BF16)


def _s5_param_specs(idx):
    def vec(width):
        return pl.BlockSpec((1, 1, width), lambda *g: (idx(*g), 0, 0))

    def rows(shape):
        return pl.BlockSpec(shape, lambda *g: (idx(*g), 0))

    return [vec(S5_ST), vec(S5_ST), vec(S5_ST),
            rows((S5_ST, SSM_GROUP)), rows((S5_ST, SSM_GROUP)),
            rows((S5_CH, SSM_STATE)), rows((S5_CH, SSM_STATE)), vec(S5_CH)]


def s5_prompt(u, s5p):
    return pl.pallas_call(
        _s5_prompt_kernel,
        out_shape=(jax.ShapeDtypeStruct((P_ROWS, SSM_WIDTH), F32),
                   jax.ShapeDtypeStruct((P_ROWS, SSM_WIDTH), BF16),
                   jax.ShapeDtypeStruct((BATCH, 1, SSM_GROUPS * SSM_STATE), F32),
                   jax.ShapeDtypeStruct((BATCH, 1, SSM_GROUPS * SSM_STATE), F32)),
        grid=(S5_NB, BATCH),
        in_specs=[pl.BlockSpec((SEQ, S5_CH), lambda j, b: (b, j))] + _s5_param_specs(lambda j, b: j),
        out_specs=(pl.BlockSpec((SEQ, S5_CH), lambda j, b: (b, j)),
                   pl.BlockSpec((SEQ, S5_CH), lambda j, b: (b, j)),
                   pl.BlockSpec((1, 1, S5_ST), lambda j, b: (b, 0, j)),
                   pl.BlockSpec((1, 1, S5_ST), lambda j, b: (b, 0, j))),
        scratch_shapes=[
            pltpu.VMEM((S5_CH, 2 * S5_ST), BF16),
            pltpu.VMEM((2 * S5_ST, S5_CH), BF16),
            pltpu.VMEM((S5_LH, SEQ, V7X_LANES), F32),
            pltpu.VMEM((S5_LH, SEQ, V7X_LANES), F32),
            pltpu.VMEM((SEQ, S5_CH), F32),
            pltpu.VMEM((SEQ, 2 * S5_ST), F32),
            pltpu.VMEM((2, S5_CR, 2 * S5_ST), F32),
        ],
        compiler_params=_params(2),
        name="s5_prompt",
    )(u, *s5p)


def _s5_sample_kernel(u_ref, h0re_ref, h0im_ref, lr_ref, li_ref, ldt_ref, bdre_ref, bdim_ref,
                      cdre_ref, cdim_ref, d_ref, sg_ref, sgb_ref, hre_ref, him_ref, bb_ref, cc_ref):
    a_re, a_im, f_re, f_im = _s5_discretize(lr_ref, li_ref, ldt_ref)
    _s5_fill_weights(f_re, f_im, bdre_ref, bdim_ref, cdre_ref, cdim_ref, bb_ref, cc_ref)
    u = u_ref[...]
    bu = _dot(u.astype(BF16), bb_ref[...])
    h0r, h0i = h0re_ref[...], h0im_ref[...]
    hr = a_re * h0r - a_im * h0i + bu[:, 0:S5_ST]
    hi = a_re * h0i + a_im * h0r + bu[:, S5_ST:2 * S5_ST]
    hre_ref[...] = hr
    him_ref[...] = hi
    y = _dot(hr.astype(BF16), cc_ref[0:S5_ST, :]) + _dot(hi.astype(BF16), cc_ref[S5_ST:2 * S5_ST, :])
    sg = jax.nn.gelu(y + d_ref[0] * u)
    sg_ref[...] = sg
    sgb_ref[...] = sg.astype(BF16)


def s5_sample(u, h0_re, h0_im, s5p):
    st = pl.BlockSpec((S_ROWS, S5_ST), lambda j: (0, j))
    return pl.pallas_call(
        _s5_sample_kernel,
        out_shape=(jax.ShapeDtypeStruct((S_ROWS, SSM_WIDTH), F32),
                   jax.ShapeDtypeStruct((S_ROWS, SSM_WIDTH), BF16),
                   jax.ShapeDtypeStruct((S_ROWS, SSM_GROUPS * SSM_STATE), F32),
                   jax.ShapeDtypeStruct((S_ROWS, SSM_GROUPS * SSM_STATE), F32)),
        grid=(S5_NB,),
        in_specs=[pl.BlockSpec((S_ROWS, S5_CH), lambda j: (0, j)), st, st]
        + _s5_param_specs(lambda j: j),
        out_specs=(pl.BlockSpec((S_ROWS, S5_CH), lambda j: (0, j)),
                   pl.BlockSpec((S_ROWS, S5_CH), lambda j: (0, j)), st, st),
        scratch_shapes=[
            pltpu.VMEM((S5_CH, 2 * S5_ST), BF16),
            pltpu.VMEM((2 * S5_ST, S5_CH), BF16),
        ],
        compiler_params=_params(1),
        name="s5_sample",
    )(u, h0_re, h0_im, *s5p)


def s5_block_params(lam_re, lam_im, log_dt, b_re, b_im, c_re, c_im, d_skip):
    n_state = SSM_GROUPS * SSM_STATE
    return (lam_re.reshape(S5_NB, 1, S5_ST), lam_im.reshape(S5_NB, 1, S5_ST),
            jnp.repeat(log_dt, SSM_STATE).reshape(S5_NB, 1, S5_ST),
            b_re.reshape(n_state, SSM_GROUP), b_im.reshape(n_state, SSM_GROUP),
            c_re.reshape(SSM_WIDTH, SSM_STATE), c_im.reshape(SSM_WIDTH, SSM_STATE),
            d_skip.reshape(S5_NB, 1, S5_CH))


SSM_GLU_TN = 512


def _ssm_glu_kernel(ap_hbm, as_ref, w_ref, tp_ref, ts_ref, op_ref, os_ref, wc_ref, r_ref, ap_ref, sem):
    tile = _RowTile([ap_hbm], [ap_ref], [sem], SSM_WIDTH // SSM_GLU_TN)

    def body(col):
        wc_ref[...] = w_ref[...].astype(BF16)

        def mm(k):
            r_ref[_sub_rows(k), :] = _dot(ap_ref[_sub_rows(k), :], wc_ref[...])

        def epi(k):
            op_ref[_sub_rows(k), :] = (
                tp_ref[_sub_rows(k), :] * jax.nn.sigmoid(r_ref[_sub_rows(k), :])).astype(op_ref.dtype)

        _interleave(NSUB, tile.mm_hooks(col, mm), epi)

        @_on_last_row_tile
        def _():
            z = _dot(as_ref[...], wc_ref[...])
            os_ref[...] = (ts_ref[...] * jax.nn.sigmoid(z)).astype(os_ref.dtype)

    tile.run(body)


def ssm_glu(sg_p, sg_s, sgb_p, sgb_s, w_glu):
    tn = SSM_GLU_TN
    return pl.pallas_call(
        _ssm_glu_kernel,
        out_shape=(jax.ShapeDtypeStruct((P_ROWS, SSM_WIDTH), BF16),
                   jax.ShapeDtypeStruct((S_ROWS, SSM_WIDTH), BF16)),
        grid=(NPT, SSM_WIDTH // tn),
        in_specs=[
            _ANY,
            _resident((S_ROWS, SSM_WIDTH), lambda i, j: (0, 0)),
            pl.BlockSpec((SSM_WIDTH, tn), lambda i, j: (0, j)),
            pl.BlockSpec((TM, tn), lambda i, j: (i, j)),
            pl.BlockSpec((S_ROWS, tn), lambda i, j: (0, j)),
        ],
        out_specs=(pl.BlockSpec((TM, tn), lambda i, j: (i, j)),
                   pl.BlockSpec((S_ROWS, tn), _sample_cols)),
        scratch_shapes=[pltpu.VMEM((SSM_WIDTH, tn), BF16), pltpu.VMEM((TM, tn), F32)]
        + _row_tile_scratch([SSM_WIDTH]),
        compiler_params=_params(2),
        name="ssm_glu",
    )(sgb_p, sgb_s, w_glu, sg_p, sg_s)


MERGE_TN = 256


def _merge_kernel(cp_hbm, cs_ref, yp_hbm, ys_ref, wa_ref, wb_ref, gap_ref, gbp_ref, gas_ref, gbs_ref,
                  op_ref, os_ref, wc_ref, r_ref, cp_ref, yp_ref, csem, ysem):
    tile = _RowTile([cp_hbm, yp_hbm], [cp_ref, yp_ref], [csem, ysem], D_MODEL // MERGE_TN)

    def body(col):
        wc_ref[0] = wa_ref[...].astype(BF16)
        wc_ref[1] = wb_ref[...].astype(BF16)

        def mm(k):
            r_ref[0, _sub_rows(k), :] = _dot(cp_ref[_sub_rows(k), :], wc_ref[0])
            r_ref[1, _sub_rows(k), :] = _dot(yp_ref[_sub_rows(k), :], wc_ref[1])

        def epi(k):
            rows = _sub_rows(k)
            op_ref[rows, :] = (gap_ref[rows, :] * r_ref[0, rows, :]
                               + gbp_ref[rows, :] * r_ref[1, rows, :]).astype(op_ref.dtype)

        _interleave(NSUB, tile.mm_hooks(col, mm), epi)

        @_on_last_row_tile
        def _():
            ya = _dot(cs_ref[...], wc_ref[0])
            yb = _dot(ys_ref[...], wc_ref[1])
            os_ref[...] = (gas_ref[...] * ya + gbs_ref[...] * yb).astype(os_ref.dtype)

    tile.run(body)


def merge(c_p, c_s, yg_p, yg_s, w_conv_out, w_ssm_out, gates_p, gates_s):
    tn = MERGE_TN
    nj = D_MODEL // tn
    return pl.pallas_call(
        _merge_kernel,
        out_shape=(jax.ShapeDtypeStruct((P_ROWS, D_MODEL), BF16),
                   jax.ShapeDtypeStruct((S_ROWS, D_MODEL), BF16)),
        grid=(NPT, nj),
        in_specs=[
            _ANY,
            _resident((S_ROWS, CONV_WIDTH), lambda i, j: (0, 0)),
            _ANY,
            _resident((S_ROWS, SSM_WIDTH), lambda i, j: (0, 0)),
            pl.BlockSpec((CONV_WIDTH, tn), lambda i, j: (0, j)),
            pl.BlockSpec((SSM_WIDTH, tn), lambda i, j: (0, j)),
            pl.BlockSpec((TM, tn), lambda i, j: (i, j)),
            pl.BlockSpec((TM, tn), lambda i, j: (i, j + nj)),
            pl.BlockSpec((S_ROWS, tn), lambda i, j: (0, j)),
            pl.BlockSpec((S_ROWS, tn), lambda i, j: (0, j + nj)),
        ],
        out_specs=(pl.BlockSpec((TM, tn), lambda i, j: (i, j)),
                   pl.BlockSpec((S_ROWS, tn), _sample_cols)),
        scratch_shapes=[pltpu.VMEM((2, CONV_WIDTH, tn), BF16), pltpu.VMEM((2, TM, tn), F32)]
        + _row_tile_scratch([CONV_WIDTH, SSM_WIDTH]),
        compiler_params=_params(2),
        name="merge",
    )(c_p, c_s, yg_p, yg_s, w_conv_out, w_ssm_out, gates_p, gates_p, gates_s, gates_s)


OPROJ_TN = 512


def _oproj_kernel(ap_hbm, as_ref, w_ref, xp_ref, xs_ref, op_ref, os_ref, wc_ref, ap_ref, sem):
    tile = _RowTile([ap_hbm], [ap_ref], [sem], D_MODEL // OPROJ_TN)

    def body(col):
        wc_ref[...] = w_ref[...].astype(BF16)

        def mm(k):
            op_ref[_sub_rows(k), :] = _dot(ap_ref[_sub_rows(k), :], wc_ref[...])

        def epi(k):
            op_ref[_sub_rows(k), :] = op_ref[_sub_rows(k), :] + xp_ref[_sub_rows(k), :]

        _interleave(NSUB, tile.mm_hooks(col, mm), epi)

        @_on_last_row_tile
        def _():
            os_ref[...] = xs_ref[...] + _dot(as_ref[...], wc_ref[...])

    tile.run(body)


def oproj(merged_p, merged_s, w_o, xp, xs):
    tn = OPROJ_TN
    return pl.pallas_call(
        _oproj_kernel,
        out_shape=(jax.ShapeDtypeStruct((P_ROWS, D_MODEL), F32),
                   jax.ShapeDtypeStruct((S_ROWS, D_MODEL), F32)),
        grid=(NPT, D_MODEL // tn),
        in_specs=[
            _ANY,
            _resident((S_ROWS, D_MODEL), lambda i, j: (0, 0)),
            pl.BlockSpec((D_MODEL, tn), lambda i, j: (0, j)),
            pl.BlockSpec((TM, tn), lambda i, j: (i, j)),
            pl.BlockSpec((S_ROWS, tn), lambda i, j: (0, j)),
        ],
        out_specs=(pl.BlockSpec((TM, tn), lambda i, j: (i, j)),
                   pl.BlockSpec((S_ROWS, tn), _sample_cols)),
        scratch_shapes=[pltpu.VMEM((D_MODEL, tn), BF16)] + _row_tile_scratch([D_MODEL]),
        compiler_params=_params(2),
        name="oproj",
    )(merged_p, merged_s, w_o, xp, xs)


DOWN_SUBS = (128, 128, 128, 128)
DOWN_TM = sum(DOWN_SUBS)
DOWN_NPT = P_ROWS // DOWN_TM
DOWN_TN = 512
DOWN_NORM_ROWS = 64


def _ffn_down_kernel(ap_hbm, as_ref, w_ref, xp_ref, xs_ref, g_ref, yp_ref, os_ref, ap_ref, sem):
    tile = _RowTile([ap_hbm], [ap_ref], [sem], D_MODEL // DOWN_TN, sizes=DOWN_SUBS)
    cols = pl.ds(pl.multiple_of(pl.program_id(1) * DOWN_TN, DOWN_TN), DOWN_TN)

    def body(col):
        def mm(k):
            rows = tile.rows(k)
            yp_ref[rows, cols] = xp_ref[rows, :] + _dot(ap_ref[rows, :], w_ref[...])

        if col == "mid":
            yp_ref[:, cols] = xp_ref[...] + _dot(ap_ref[...], w_ref[...])
        else:
            for k in range(len(DOWN_SUBS)):
                tile.mm_hooks(col, mm)(k)

        @pl.when(pl.program_id(0) == DOWN_NPT - 1)
        def _():
            os_ref[...] = xs_ref[...] + _dot(as_ref[...], w_ref[...])

        if col == "last":
            def norm(c, carry):
                rows = pl.ds(pl.multiple_of(c * DOWN_NORM_ROWS, DOWN_NORM_ROWS), DOWN_NORM_ROWS)
                yp_ref[rows, :] = _rms(yp_ref[rows, :], g_ref[...])
                return carry

            lax.fori_loop(0, DOWN_TM // DOWN_NORM_ROWS, norm, 0)

    tile.run(body)


def ffn_down(h_p, h_s, w_down_bf16, x1_p, x1_s, final_g):
    tn = DOWN_TN
    last = DOWN_NPT - 1
    return pl.pallas_call(
        _ffn_down_kernel,
        out_shape=(jax.ShapeDtypeStruct((P_ROWS, D_MODEL), F32),
                   jax.ShapeDtypeStruct((S_ROWS, D_MODEL), F32)),
        grid=(DOWN_NPT, D_MODEL // tn),
        in_specs=[
            _ANY,
            _resident((S_ROWS, D_FF), lambda i, j: (0, 0)),
            pl.BlockSpec((D_FF, tn), lambda i, j: (0, j)),
            pl.BlockSpec((DOWN_TM, tn), lambda i, j: (i, j)),
            pl.BlockSpec((S_ROWS, tn), lambda i, j: (0, j)),
            pl.BlockSpec((1, D_MODEL), lambda i, j: (0, 0)),
        ],
        out_specs=(pl.BlockSpec((DOWN_TM, D_MODEL), lambda i, j: (i, 0)),
                   pl.BlockSpec((S_ROWS, tn), lambda i, j: (0, jnp.where(i == last, j, 0)))),
        scratch_shapes=_row_tile_scratch([D_FF], sizes=DOWN_SUBS),
        compiler_params=_params(2),
        name="ffn_down",
    )(h_p, h_s, w_down_bf16, x1_p, x1_s, final_g)


FFN_TN = 256
FFN_NJ = D_FF // FFN_TN
FFN_PAD = V7X_SUBLANES


def _ffn_up_kernel(ap_hbm, as_ref, wg_ref, wv_ref, s0_ref, s1_ref, cw_ref, cb_ref, wd_ref,
                   hp_ref, hs_ref, tail_ref, gs_ref, wdb_ref, wc_ref, g_ref, v_ref, ap_ref, sem):
    tn = FFN_TN
    tile = _RowTile([ap_hbm], [ap_ref], [sem], FFN_NJ)

    def body(col):
        wc_ref[:, 0:tn] = wg_ref[...].astype(BF16)
        wc_ref[:, tn:2 * tn] = wv_ref[...].astype(BF16)
        wdb_ref[...] = wd_ref[...].astype(BF16)
        cols = pl.ds(pl.multiple_of(pl.program_id(1) * tn, tn), tn)
        w0, w1, w2 = cw_ref[0:1, cols], cw_ref[1:2, cols], cw_ref[2:3, cols]
        cb = cb_ref[:, cols]
        g_ref[0:FFN_PAD, :] = jnp.zeros((FFN_PAD, tn), F32)

        def mm(k):
            r = _dot(ap_ref[_sub_rows(k), :], wc_ref[...])
            g_ref[_sub_rows(k, FFN_PAD), :] = r[:, 0:tn]
            v_ref[_sub_rows(k), :] = r[:, tn:2 * tn]

        def epi(k):
            gc = (w0 * g_ref[_sub_rows(k, FFN_PAD - 2), :] + w1 * g_ref[_sub_rows(k, FFN_PAD - 1), :]
                  + w2 * g_ref[_sub_rows(k, FFN_PAD), :] + cb)
            hp_ref[_sub_rows(k), :] = (gc * jax.nn.sigmoid(gc) * v_ref[_sub_rows(k), :]).astype(hp_ref.dtype)

        _interleave(NSUB, tile.mm_hooks(col, mm), epi)
        tail_ref[0, :, cols] = g_ref[TM:TM + FFN_PAD, :]

        @_on_last_row_tile
        def _():
            r = _dot(as_ref[...], wc_ref[...])
            gate, val = r[:, 0:tn], r[:, tn:2 * tn]
            gc = w0 * s0_ref[...] + w1 * s1_ref[...] + w2 * gate + cb
            hs_ref[...] = (gc * jax.nn.sigmoid(gc) * val).astype(hs_ref.dtype)
            gs_ref[...] = gate

    tile.run(body)


def ffn_up(xn2_p, xn2_s, w_up, ffn_old, ffn_conv_w, ffn_conv_b, w_down):
    nj = FFN_NJ
    tn = FFN_TN
    wd_rows = D_FF // (NPT * nj)

    def wd_slab(i, j):
        return (i * nj + j, 0)

    whole_row = lambda i, j: (0, 0)

    return pl.pallas_call(
        _ffn_up_kernel,
        out_shape=(jax.ShapeDtypeStruct((P_ROWS, D_FF), BF16),
                   jax.ShapeDtypeStruct((S_ROWS, D_FF), BF16),
                   jax.ShapeDtypeStruct((NPT, FFN_PAD, D_FF), F32),
                   jax.ShapeDtypeStruct((S_ROWS, D_FF), F32),
                   jax.ShapeDtypeStruct((D_FF, D_MODEL), BF16)),
        grid=(NPT, nj),
        in_specs=[
            _ANY,
            _resident((S_ROWS, D_MODEL), lambda i, j: (0, 0)),
            pl.BlockSpec((D_MODEL, tn), lambda i, j: (0, j)),
            pl.BlockSpec((D_MODEL, tn), lambda i, j: (0, j + nj)),
            pl.BlockSpec((S_ROWS, tn), _sample_cols),
            pl.BlockSpec((S_ROWS, tn), _sample_cols),
            pl.BlockSpec((FFN_K, D_FF), whole_row),
            pl.BlockSpec((1, D_FF), whole_row),
            pl.BlockSpec((wd_rows, D_MODEL), wd_slab),
        ],
        out_specs=(
            pl.BlockSpec((TM, tn), lambda i, j: (i, j)),
            pl.BlockSpec((S_ROWS, tn), _sample_cols),
            pl.BlockSpec((1, FFN_PAD, D_FF), lambda i, j: (i, 0, 0)),
            pl.BlockSpec((S_ROWS, tn), _sample_cols),
            pl.BlockSpec((wd_rows, D_MODEL), wd_slab),
        ),
        scratch_shapes=[
            pltpu.VMEM((D_MODEL, 2 * tn), BF16),
            pltpu.VMEM((FFN_PAD + TM, tn), F32),
            pltpu.VMEM((TM, tn), F32),
        ] + _row_tile_scratch([D_MODEL]),
        compiler_params=_params(2),
        name="ffn_up",
    )(xn2_p, xn2_s, w_up, w_up, ffn_old[0], ffn_old[1], ffn_conv_w, ffn_conv_b, w_down)


def kernel(x_prompt, x_sample, state_conv, state_ssm_re, state_ssm_im, state_ffn_conv,
           norm_mix_g, w_in, conv_w, conv_b, ln_g, ln_b, w_conv_out,
           lam_re, lam_im, log_dt, b_re, b_im, c_re, c_im, d_skip, w_glu, w_ssm_out, w_o,
           norm_ffn_g, w_up, ffn_conv_w, ffn_conv_b, w_down, final_norm_g):
    xp = x_prompt.reshape(P_ROWS, D_MODEL)
    xs = x_sample.reshape(S_ROWS, D_MODEL)

    def row(v):
        return v.reshape(1, -1)

    xn_p, xn_s = rownorm_pair(xp, xs, row(norm_mix_g[0]), BF16)
    glu_p, glu_s = inproj_glu(xn_p, xn_s, w_in[0])
    u_p, u_s = inproj_cols(xn_p, xn_s, w_in[0], 2 * CONV_WIDTH, SSM_WIDTH, False, "inproj_ssm")
    gates_p, gates_s = inproj_cols(xn_p, xn_s, w_in[0], 2 * CONV_WIDTH + SSM_WIDTH, 2 * D_MODEL, True,
                                   "inproj_gates")

    conv_vecs = (conv_w[0], row(conv_b[0]), row(ln_g[0]), row(ln_b[0]))
    c_p, conv_p = conv_prompt(glu_p, *conv_vecs)
    c_s, conv_s = conv_sample(glu_s, state_conv[0].transpose(1, 0, 2), *conv_vecs)
    conv_s = conv_s.transpose(1, 0, 2)

    s5p = s5_block_params(lam_re[0], lam_im[0], log_dt[0], b_re[0], b_im[0], c_re[0], c_im[0], d_skip[0])
    n_state = SSM_GROUPS * SSM_STATE
    sg_p, sgb_p, ssr_p, ssi_p = s5_prompt(u_p, s5p)
    sg_s, sgb_s, ssr_s, ssi_s = s5_sample(u_s, state_ssm_re[0].reshape(S_ROWS, n_state),
                                          state_ssm_im[0].reshape(S_ROWS, n_state), s5p)
    yg_p, yg_s = ssm_glu(sg_p, sg_s, sgb_p, sgb_s, w_glu[0])

    merged_p, merged_s = merge(c_p, c_s, yg_p, yg_s, w_conv_out[0], w_ssm_out[0], gates_p, gates_s)
    x1_p, x1_s = oproj(merged_p, merged_s, w_o[0], xp, xs)

    xn2_p, xn2_s = rownorm_pair(x1_p, x1_s, row(norm_ffn_g[0]), BF16)
    ffn_old = [state_ffn_conv[0, :, k, :] for k in range(FFN_K - 1)]
    h_p, h_s, gate_tail, gate_s, w_down_bf16 = ffn_up(
        xn2_p, xn2_s, w_up[0], ffn_old, ffn_conv_w[0], row(ffn_conv_b[0]), w_down[0])
    y_p, x2_s = ffn_down(h_p, h_s, w_down_bf16, x1_p, x1_s, row(final_norm_g))
    y_s = rownorm(x2_s, row(final_norm_g), F32, S_ROWS)

    ffn_p = gate_tail[:, FFN_PAD - (FFN_K - 1):, :]
    ffn_s = jnp.stack([ffn_old[1], gate_s], axis=1)
    state_shape = (1, -1, SSM_GROUPS, SSM_STATE)
    return (y_p.reshape(BATCH, SEQ, D_MODEL), y_s.reshape(DEC_BATCH, 1, D_MODEL),
            conv_p[None], conv_s[None],
            ssr_p.reshape(state_shape), ssi_p.reshape(state_shape),
            ssr_s.reshape(state_shape), ssi_s.reshape(state_shape),
            ffn_p[None], ffn_s[None])
```
